```python
import math
import jax, jax.numpy as jnp
from jax import lax
import numpy as np

D_MODEL = 1024
BATCH = 4
SEQ = 8192
DEPTH = 1
DEC_BATCH = 16
DEC_SEQ = 4096
PAST_LEN = 128

SSD_WIDTH = D_MODEL
ATTN_WIDTH = D_MODEL
MIX_WIDTH = SSD_WIDTH + ATTN_WIDTH
SSD_HEAD_DIM = 64
N_SSD_HEADS = SSD_WIDTH // SSD_HEAD_DIM
N_BC_GROUPS = 2
D_STATE = 128
D_CONV = 5
CHUNK = 128
CONV_DIM = SSD_WIDTH + 2 * N_BC_GROUPS * D_STATE
ATTN_HEAD_DIM = 64
N_ATTN_HEADS = ATTN_WIDTH // ATTN_HEAD_DIM
GRID_W = 64
MAX_WIN_H = 8
WIN_W = 16
Q_COL_BLOCK = 16
K_COL_BLOCK = Q_COL_BLOCK + WIN_W
N_COL_BLOCKS = GRID_W // Q_COL_BLOCK
IN_SIZES = (SSD_WIDTH, CONV_DIM, 2 * N_SSD_HEADS, ATTN_WIDTH, ATTN_WIDTH, ATTN_WIDTH)
IN_WIDTH = sum(IN_SIZES)
N_GROUPS = 4
EXPERTS_PER_GROUP = 8
N_EXPERTS = N_GROUPS * EXPERTS_PER_GROUP
TOP_K = 2
D_EXPERT = 512
EXPERT_BLOCK = 256
PLE_DIM = 256
EPS = 1e-6

kernel_name = 'hymba_ssd_natten_hmoe_encoder'


def rms_norm(x, g):
    xf = x.astype(jnp.float32)
    y = xf * lax.rsqrt(jnp.mean(xf * xf, axis=-1, keepdims=True) + EPS)
    return (y * g.astype(jnp.float32)).astype(x.dtype)


def ssd_chunked(x, dt, a, bm, cm):
    b_, l, h, p = x.shape
    g, n = bm.shape[2], bm.shape[3]
    j = h // g
    c = l // CHUNK
    da = (dt * a).reshape(b_, c, CHUNK, g, j).transpose(0, 3, 4, 1, 2)
    xdt = (x * dt[..., None]).reshape(b_, c, CHUNK, g, j, p)
    bm = bm.reshape(b_, c, CHUNK, g, n)
    cm = cm.reshape(b_, c, CHUNK, g, n)
    cs = jnp.cumsum(da, axis=-1)
    lower = np.tril(np.ones((CHUNK, CHUNK), dtype=bool))
    seg = jnp.exp(jnp.where(lower, cs[..., :, None] - cs[..., None, :], -jnp.inf))
    cb = jnp.einsum('bclgn,bcsgn->bgcls', cm, bm)
    y_diag = jnp.einsum('bgjcls,bcsgjp->bclgjp', cb[:, :, None] * seg, xdt)
    decay_in = jnp.exp(cs[..., -1:] - cs)
    states = jnp.einsum('bcsgn,bgjcs,bcsgjp->cbgjpn', bm, decay_in, xdt)
    chunk_decay = jnp.exp(cs[..., -1]).transpose(3, 0, 1, 2)

    def step(carry, inp):
        st, dec = inp
        return carry * dec[..., None, None] + st, carry

    _, prev = lax.scan(step, jnp.zeros_like(states[0]), (states, chunk_decay))
    y_off = jnp.einsum('bclgn,cbgjpn,bgjcl->bclgjp', cm, prev, jnp.exp(cs))
    return (y_diag + y_off).reshape(b_, l, h, p)


def ssd_group(z, xbc, dt_raw, conv_w, conv_b, dt_bias, a_log, d_skip, ssd_norm):
    b_, l, _ = xbc.shape
    xbc = lax.conv_general_dilated(
        xbc, conv_w[:, None, :].astype(xbc.dtype), window_strides=(1,),
        padding=[(D_CONV // 2, D_CONV // 2)], dimension_numbers=('NWC', 'WIO', 'NWC'),
        feature_group_count=CONV_DIM)
    xbc = jax.nn.silu(xbc + conv_b).astype(jnp.float32)
    xs = xbc[..., :SSD_WIDTH].reshape(b_, l, N_SSD_HEADS, SSD_HEAD_DIM)
    bm = xbc[..., SSD_WIDTH:SSD_WIDTH + N_BC_GROUPS * D_STATE].reshape(b_, l, N_BC_GROUPS, D_STATE)
    cm = xbc[..., SSD_WIDTH + N_BC_GROUPS * D_STATE:].reshape(b_, l, N_BC_GROUPS, D_STATE)
    dt = jax.nn.softplus(dt_raw.astype(jnp.float32).reshape(b_, l, 2, N_SSD_HEADS)
                         + dt_bias.astype(jnp.float32))
    a = -jnp.exp(a_log.astype(jnp.float32))
    flip = lambda t: jnp.flip(t, axis=1)
    y_fwd = ssd_chunked(xs, dt[:, :, 0], a[0], bm, cm)
    y_bwd = flip(ssd_chunked(flip(xs), flip(dt[:, :, 1]), a[1], flip(bm), flip(cm)))
    y = y_fwd + y_bwd + d_skip.astype(jnp.float32)[:, None] * xs
    y = y.reshape(b_, l, SSD_WIDTH) * jax.nn.silu(z.astype(jnp.float32))
    return rms_norm(y, ssd_norm)


def _col_tables():
    qc = np.arange(GRID_W).reshape(N_COL_BLOCKS, Q_COL_BLOCK)
    s_c = np.clip(qc - WIN_W // 2, 0, GRID_W - WIN_W)
    kc0 = np.clip(np.arange(N_COL_BLOCKS) * Q_COL_BLOCK - WIN_W // 2, 0, GRID_W - K_COL_BLOCK)
    kcols = kc0[:, None] + np.arange(K_COL_BLOCK)[None, :]
    mask = (kcols[:, None, :] >= s_c[:, :, None]) & (kcols[:, None, :] < s_c[:, :, None] + WIN_W)
    dcol = np.clip(kcols[:, None, :] - qc[:, :, None], -(WIN_W - 1), WIN_W - 1) + WIN_W - 1
    return kcols, mask, dcol


def _row_tables(rows, win_h):
    r = np.arange(rows)
    s_r = np.clip(r - win_h // 2, 0, rows - win_h)
    drow = s_r[:, None] + np.arange(win_h)[None, :] - r[:, None] + MAX_WIN_H - 1
    return s_r.astype(np.int32), drow.astype(np.int32)


def neighbourhood_attention(q, k, v, rel_bias):
    b_, l, h, d = q.shape
    rows = l // GRID_W
    win_h = min(MAX_WIN_H, rows)
    kcols, col_mask, dcol = _col_tables()
    s_r, drow = _row_tables(rows, win_h)
    bias_cols = rel_bias.astype(jnp.float32)[:, :, dcol]
    bias_masked = jnp.where(col_mask[None, None], bias_cols, -jnp.inf)
    k_grid = k.reshape(b_, rows, GRID_W, h, d)
    v_grid = v.reshape(b_, rows, GRID_W, h, d)
    q_rows = q.reshape(b_, rows, N_COL_BLOCKS, Q_COL_BLOCK, h, d).transpose(1, 0, 2, 3, 4, 5)
    scale = ATTN_HEAD_DIM ** -0.5

    def row_block(args):
        q_r, s, dr = args
        kb = lax.dynamic_slice_in_dim(k_grid, s, win_h, axis=1)[:, :, kcols]
        vb = lax.dynamic_slice_in_dim(v_grid, s, win_h, axis=1)[:, :, kcols]
        bias = bias_masked[:, dr].transpose(0, 2, 3, 1, 4)
        sc = jnp.einsum('bmqhd,bimkhd->bhmqik', q_r, kb).astype(jnp.float32) * scale + bias[None]
        pr = jax.nn.softmax(sc.reshape(sc.shape[:4] + (win_h * K_COL_BLOCK,)), axis=-1)
        pr = pr.reshape(sc.shape).astype(vb.dtype)
        return jnp.einsum('bhmqik,bimkhd->bmqhd', pr, vb)

    out = lax.map(row_block, (q_rows, jnp.asarray(s_r), jnp.asarray(drow)))
    return out.transpose(1, 0, 2, 3, 4, 5).reshape(b_, l, h * d)


def hybrid_mixer(u, w_in, conv_w, conv_b, dt_bias, a_log, d_skip, ssd_norm,
                 rel_bias, attn_norm, w_out):
    b_, l, _ = u.shape
    proj = u @ w_in
    split_at = np.cumsum(IN_SIZES)[:-1].tolist()
    z, xbc, dt_raw, q, k, v = jnp.split(proj, split_at, axis=-1)
    y_ssd = ssd_group(z, xbc, dt_raw, conv_w, conv_b, dt_bias, a_log, d_skip, ssd_norm)
    hs = (b_, l, N_ATTN_HEADS, ATTN_HEAD_DIM)
    y_att = neighbourhood_attention(q.reshape(hs), k.reshape(hs), v.reshape(hs), rel_bias)
    y_att = rms_norm(y_att, attn_norm)
    y = jnp.concatenate([y_ssd.astype(u.dtype), y_att.astype(u.dtype)], axis=-1)
    return y @ w_out


def grouped_expert_mlp(x, expert, gate, w_gate, w_up, w_down):
    n, dm = x.shape
    kk = expert.shape[1]
    a = n * kk
    flat_e = expert.reshape(-1)
    flat_tok = jnp.repeat(jnp.arange(n, dtype=jnp.int32), kk)
    order = jnp.argsort(flat_e, stable=True)
    sorted_e = flat_e[order]
    counts = jnp.bincount(flat_e, length=N_EXPERTS)
    padded = (counts + EXPERT_BLOCK - 1) // EXPERT_BLOCK * EXPERT_BLOCK
    pad_end = jnp.cumsum(padded)
    pad_start = pad_end - padded
    start = jnp.cumsum(counts) - counts
    dest = pad_start[sorted_e] + jnp.arange(a, dtype=jnp.int32) - start[sorted_e]
    n_blocks = -(-(a + N_EXPERTS * (EXPERT_BLOCK - 1)) // EXPERT_BLOCK)
    slots = n_blocks * EXPERT_BLOCK
    sorted_tok = flat_tok[order]
    slot_tok = jnp.full((slots,), n, dtype=jnp.int32).at[dest].set(sorted_tok)
    x_pad = jnp.concatenate([x, jnp.zeros((1, dm), x.dtype)], axis=0)
    xb = x_pad[slot_tok].reshape(n_blocks, EXPERT_BLOCK, dm)
    blk_e = jnp.minimum(jnp.searchsorted(pad_end, jnp.arange(n_blocks) * EXPERT_BLOCK, side='right'),
                        N_EXPERTS - 1)

    def block_mlp(args):
        xblk, e = args
        hid = jax.nn.silu(xblk @ w_gate[e]) * (xblk @ w_up[e])
        return hid @ w_down[e]

    yb = lax.map(block_mlp, (xb, blk_e)).reshape(slots, dm)
    y_sorted = yb[dest] * gate.reshape(-1)[order][:, None]
    return jax.ops.segment_sum(y_sorted, sorted_tok, num_segments=n).astype(x.dtype)


def hier_moe(u, router_group, router_expert, w_gate, w_up, w_down):
    b_, l, dm = u.shape
    x = u.reshape(-1, dm)
    g_probs = jax.nn.softmax((x @ router_group).astype(jnp.float32), axis=-1)
    g_val, g_idx = lax.top_k(g_probs, 1)
    e_logits = jnp.einsum('nd,gde->nge', x, router_expert).astype(jnp.float32)
    e_logits = jnp.take_along_axis(e_logits, g_idx[:, :, None], axis=1)[:, 0]
    e_val, e_idx = lax.top_k(e_logits, TOP_K)
    gate = g_val * jax.nn.softmax(e_val, axis=-1)
    expert = g_idx * EXPERTS_PER_GROUP + e_idx
    y = grouped_expert_mlp(x, expert, gate, w_gate, w_up, w_down)
    return y.reshape(b_, l, dm)


def encoder(x, p, w_in, conv_w, conv_b, dt_bias, a_log, d_skip, ssd_norm, rel_bias,
            attn_norm, w_out, norm_mix, norm_ffn, router_group, router_expert, w_gate,
            w_up, w_down, norm_ple, w_ple_proj, w_ple_gate, norm_final):
    h = x
    for i in range(DEPTH):
        u = rms_norm(h, norm_mix[i])
        h = h + hybrid_mixer(u, w_in[i], conv_w[i], conv_b[i], dt_bias[i], a_log[i], d_skip[i],
                             ssd_norm[i], rel_bias[i], attn_norm[i], w_out[i])
        u = rms_norm(h, norm_ffn[i])
        h = h + hier_moe(u, router_group[i], router_expert[i], w_gate[i], w_up[i], w_down[i])
        u = rms_norm(h, norm_ple[i])
        h = h + (p[i] @ w_ple_proj[i]) * jax.nn.sigmoid(u @ w_ple_gate[i])
    return rms_norm(h, norm_final)


def setup_inputs(seed: int = 0) -> dict:
    key = jax.random.key(seed)
    ks = jax.random.split(key, 26)
    nrm = lambda k, s, sc: jax.random.normal(k, s, jnp.float32) * sc
    dt0 = jnp.exp(jax.random.uniform(ks[7], (DEPTH, 2, N_SSD_HEADS), jnp.float32)
                  * (math.log(0.1) - math.log(0.001)) + math.log(0.001))
    return {
        'x_prompt': nrm(ks[0], (BATCH, SEQ, D_MODEL), 1.0),
        'x_sample': nrm(ks[1], (DEC_BATCH, DEC_SEQ, D_MODEL), 1.0),
        'p_prompt': nrm(ks[2], (DEPTH, BATCH, SEQ, PLE_DIM), 1.0),
        'p_sample': nrm(ks[3], (DEPTH, DEC_BATCH, DEC_SEQ, PLE_DIM), 1.0),
        'w_in': nrm(ks[4], (DEPTH, D_MODEL, IN_WIDTH), D_MODEL ** -0.5),
        'conv_w': nrm(ks[5], (DEPTH, D_CONV, CONV_DIM), D_CONV ** -0.5),
        'conv_b': nrm(ks[6], (DEPTH, CONV_DIM), 0.01),
        'dt_bias': dt0 + jnp.log(-jnp.expm1(-dt0)),
        'a_log': jnp.log(jax.random.uniform(ks[8], (DEPTH, 2, N_SSD_HEADS), jnp.float32, 1.0, 16.0)),
        'd_skip': 1.0 + nrm(ks[9], (DEPTH, N_SSD_HEADS), 0.1),
        'ssd_norm': 1.0 + nrm(ks[10], (DEPTH, SSD_WIDTH), 0.05),
        'rel_bias': nrm(ks[11], (DEPTH, N_ATTN_HEADS, 2 * MAX_WIN_H - 1, 2 * WIN_W - 1), 0.1),
        'attn_norm': 1.0 + nrm(ks[12], (DEPTH, ATTN_WIDTH), 0.05),
        'w_out': nrm(ks[13], (DEPTH, MIX_WIDTH, D_MODEL), MIX_WIDTH ** -0.5),
        'norm_mix': 1.0 + nrm(ks[14], (DEPTH, D_MODEL), 0.05),
        'norm_ffn': 1.0 + nrm(ks[15], (DEPTH, D_MODEL), 0.05),
        'router_group': nrm(ks[16], (DEPTH, D_MODEL, N_GROUPS), D_MODEL ** -0.5),
        'router_expert': nrm(ks[17], (DEPTH, N_GROUPS, D_MODEL, EXPERTS_PER_GROUP), D_MODEL ** -0.5),
        'w_gate': nrm(ks[18], (DEPTH, N_EXPERTS, D_MODEL, D_EXPERT), D_MODEL ** -0.5),
        'w_up': nrm(ks[19], (DEPTH, N_EXPERTS, D_MODEL, D_EXPERT), D_MODEL ** -0.5),
        'w_down': nrm(ks[20], (DEPTH, N_EXPERTS, D_EXPERT, D_MODEL), D_EXPERT ** -0.5),
        'norm_ple': 1.0 + nrm(ks[21], (DEPTH, D_MODEL), 0.05),
        'w_ple_proj': nrm(ks[22], (DEPTH, PLE_DIM, D_MODEL), PLE_DIM ** -0.5),
        'w_ple_gate': nrm(ks[23], (DEPTH, D_MODEL, D_MODEL), D_MODEL ** -0.5),
        'norm_final': 1.0 + nrm(ks[24], (D_MODEL,), 0.05),
    }


def reference(x_prompt, x_sample, p_prompt, p_sample, w_in, conv_w, conv_b, dt_bias, a_log,
              d_skip, ssd_norm, rel_bias, attn_norm, w_out, norm_mix, norm_ffn, router_group,
              router_expert, w_gate, w_up, w_down, norm_ple, w_ple_proj, w_ple_gate, norm_final):
    y_prompt = encoder(x_prompt, p_prompt, w_in, conv_w, conv_b, dt_bias, a_log, d_skip, ssd_norm,
                       rel_bias, attn_norm, w_out, norm_mix, norm_ffn, router_group, router_expert,
                       w_gate, w_up, w_down, norm_ple, w_ple_proj, w_ple_gate, norm_final)
    y_sample = encoder(x_sample, p_sample, w_in, conv_w, conv_b, dt_bias, a_log, d_skip, ssd_norm,
                       rel_bias, attn_norm, w_out, norm_mix, norm_ffn, router_group, router_expert,
                       w_gate, w_up, w_down, norm_ple, w_ple_proj, w_ple_gate, norm_final)
    return (y_prompt, y_sample)
```

```python
import functools
import math

import numpy as np
import jax
import jax.numpy as jnp
from jax import lax
from jax.experimental import pallas as pl
from jax.experimental.pallas import tpu as pltpu

F32 = jnp.float32
BF16 = jnp.bfloat16

D_MODEL = 1024
SSD_WIDTH = 1024
HEAD_DIM = 64
N_HEADS = 16
N_BC_GROUPS = 2
D_STATE = 128
D_CONV = 5
CHUNK = 128
CONV_DIM = SSD_WIDTH + 2 * N_BC_GROUPS * D_STATE
GRID_W = 64
WIN_H = 8
WIN_W = 16
N_GROUPS = 4
EXPERTS_PER_GROUP = 8
N_EXPERTS = N_GROUPS * EXPERTS_PER_GROUP
D_EXPERT = 512
PLE_DIM = 256
EPS = 1e-6

LANES = 128
HALO_ROWS = 16
TOKEN_TILE = 512
SSD_TILE = 512
NA_ROWS = 8
SLOT_BLOCK = 512
ROUTER_ROWS = 48
VMEM_LIMIT = 56 * 1024 * 1024


def _rms(x, g):
    ms = jnp.mean(x * x, axis=-1, keepdims=True)
    return x * lax.rsqrt(ms + EPS) * g


def _split3(x):
    hi = x.astype(BF16)
    r1 = x - hi.astype(F32)
    mid = r1.astype(BF16)
    lo = (r1 - mid.astype(F32)).astype(BF16)
    return hi, mid, lo


def _dot(a, b):
    return jnp.dot(a, b, preferred_element_type=F32)


def _dot_nt(a, b):
    return lax.dot_general(a, b, (((1,), (1,)), ((), ())), preferred_element_type=F32)


def _params(sem):
    return pltpu.CompilerParams(dimension_semantics=sem, vmem_limit_bytes=VMEM_LIMIT)


def _inproj_kernel(x_ref, g_ref, wz_ref, wxbc_ref, wq_ref, wk_ref, wv_ref, wdt_ref,
                   z_ref, xbc_ref, q_ref, k_ref, v_ref, dt_ref):
    u = _rms(x_ref[...], g_ref[...]).astype(BF16)
    z_ref[...] = _dot(u, wz_ref[...]).astype(BF16)
    xbc_ref[...] = _dot(u, wxbc_ref[...]).astype(BF16)
    q_ref[...] = (_dot(u, wq_ref[...]) * (HEAD_DIM ** -0.5)).astype(BF16)
    k_ref[...] = _dot(u, wk_ref[...]).astype(BF16)
    v_ref[...] = _dot(u, wv_ref[...]).astype(BF16)
    dt_ref[...] = _dot(u, wdt_ref[...])


def _inproj(x, g, wz, wxbc, wq, wk, wv, wdt):
    n = x.shape[0]
    tm = TOKEN_TILE
    row = lambda w: pl.BlockSpec((tm, w), lambda i: (i, 0))
    full = lambda a: pl.BlockSpec(a.shape, lambda i: (0,) * a.ndim)
    outs = [(D_MODEL, BF16), (CONV_DIM, BF16), (D_MODEL, BF16), (D_MODEL, BF16), (D_MODEL, BF16),
            (2 * LANES, F32)]
    return pl.pallas_call(
        _inproj_kernel,
        grid=(n // tm,),
        in_specs=[row(D_MODEL), full(g), full(wz), full(wxbc), full(wq), full(wk), full(wv), full(wdt)],
        out_specs=[row(w) for w, _ in outs],
        out_shape=[jax.ShapeDtypeStruct((n, w), d) for w, d in outs],
        compiler_params=_params(("parallel",)),
        name="inproj",
    )(x, g, wz, wxbc, wq, wk, wv, wdt)


def _ssd_kernel(*refs, backward, n_blocks):
    if backward:
        (xbc_ref, prev_ref, next_ref, dt_ref, cw_ref, cb_ref, dtb_ref, alog_ref,
         out_ref, xp_ref, act_ref, st_ref) = refs
    else:
        (xbc_ref, prev_ref, next_ref, dt_ref, cw_ref, cb_ref, dtb_ref, alog_ref,
         z_ref, ybwd_ref, dexp_ref, gn_ref, out_ref, xp_ref, act_ref, st_ref) = refs
    tb = xbc_ref.shape[0]
    t = CHUNK
    n_chunks = tb // t
    j = pl.program_id(1)
    blk = (n_blocks - 1 - j) if backward else j

    @pl.when(j == 0)
    def _():
        st_ref[...] = jnp.zeros_like(st_ref)

    prev_rows = prev_ref[HALO_ROWS - 8:, :].astype(F32)
    next_rows = next_ref[:8, :].astype(F32)
    xp_ref[0:8, :] = jnp.where(blk == 0, 0.0, prev_rows)
    xp_ref[8:8 + tb, :] = xbc_ref[...].astype(F32)
    xp_ref[8 + tb:, :] = jnp.where(blk == n_blocks - 1, 0.0, next_rows)

    for c in range(n_chunks):
        acc = cb_ref[...] + cw_ref[0:1, :] * xp_ref[c * t + 6:c * t + 6 + t, :]
        for kk in range(1, D_CONV):
            acc = acc + cw_ref[kk:kk + 1, :] * xp_ref[c * t + 6 + kk:c * t + 6 + kk + t, :]
        act_ref[c * t:(c + 1) * t, :] = acc * (1.0 / (1.0 + jnp.exp(-acc)))

    a_row = -jnp.exp(alog_ref[...])
    dtb = dtb_ref[...]
    ri = lax.broadcasted_iota(jnp.int32, (t, t), 0)
    ci = lax.broadcasted_iota(jnp.int32, (t, t), 1)
    if backward:
        keep = ci >= ri
    else:
        keep = ri >= ci
    tri = jnp.where(keep, 1.0, 0.0).astype(BF16)
    er = lax.broadcasted_iota(jnp.int32, (2 * LANES, SSD_WIDTH), 0)
    ec = lax.broadcasted_iota(jnp.int32, (2 * LANES, SSD_WIDTH), 1)
    expand = jnp.where((er % LANES) == (ec // HEAD_DIM), 1.0, 0.0).astype(BF16)
    lane = lax.broadcasted_iota(jnp.int32, (t, LANES), 1)
    low_half = lane < HEAD_DIM

    def chunk_body(i, carry):
        c = (n_chunks - 1 - i) if backward else i
        r0 = pl.multiple_of(c * t, t)
        act = act_ref[pl.ds(r0, t), :]
        xs = act[:, :SSD_WIDTH]
        xs_bf = xs.astype(BF16)
        bm = act[:, SSD_WIDTH:SSD_WIDTH + N_BC_GROUPS * D_STATE]
        cm = act[:, SSD_WIDTH + N_BC_GROUPS * D_STATE:]

        dt_raw = dt_ref[pl.ds(r0, t), :] + dtb
        dt = jnp.maximum(dt_raw, 0.0) + jnp.log(1.0 + jnp.exp(-jnp.abs(dt_raw)))
        da = dt * a_row
        hi, mid, lo = _split3(da)
        cs = _dot(tri, hi) + _dot(tri, mid) + _dot(tri, lo)
        cs_row = cs.T
        dt_row = dt.T
        edge = cs[0:1, :] if backward else cs[t - 1:t, :]
        w_in = dt * jnp.exp(edge - cs)
        ecs = jnp.exp(cs)
        both = jnp.concatenate([w_in, ecs], axis=0)
        b_hi = both.astype(BF16)
        b_lo = (both - b_hi.astype(F32)).astype(BF16)
        both_x = _dot(jnp.concatenate([b_hi, b_lo], axis=1), expand)
        w_x = both_x[:t]
        ecs_x = both_x[t:]
        cd_x = ecs_x[0:1, :] if backward else ecs_x[t - 1:t, :]

        st = st_ref[...]
        st_bf = st.astype(BF16)
        xw = (xs * w_x).astype(BF16)
        y_parts = []
        new_parts = []
        for g in range(N_BC_GROUPS):
            gs = slice(g * D_STATE, (g + 1) * D_STATE)
            hs = slice(g * (SSD_WIDTH // N_BC_GROUPS), (g + 1) * (SSD_WIDTH // N_BC_GROUPS))
            b_g = bm[:, gs]
            c_g = cm[:, gs].astype(BF16)
            cb = _dot_nt(c_g, b_g.astype(BF16))
            y_off = _dot(c_g, st_bf[:, hs])
            new_parts.append(_dot(b_g.T.astype(BF16), xw[:, hs]))
            heads_per_group = N_HEADS // N_BC_GROUPS
            pair_parts = []
            for pp in range(heads_per_group // 2):
                h0 = g * heads_per_group + 2 * pp
                xs_pair = xs_bf[:, h0 * HEAD_DIM:(h0 + 2) * HEAD_DIM]
                ys = []
                for h in (h0, h0 + 1):
                    diff = cs[:, h:h + 1] - cs_row[h:h + 1, :]
                    seg = jnp.exp(jnp.where(keep, diff, -jnp.inf))
                    m = (cb * seg * dt_row[h:h + 1, :]).astype(BF16)
                    ys.append(_dot(m, xs_pair))
                pair_parts.append(jnp.where(low_half, ys[0], ys[1]))
            y_diag = jnp.concatenate(pair_parts, axis=1)
            y_parts.append(y_diag + y_off * ecs_x[:, hs])
        y = jnp.concatenate(y_parts, axis=1)
        st_ref[...] = st * cd_x + jnp.concatenate(new_parts, axis=1)

        if backward:
            out_ref[pl.ds(r0, t), :] = y.astype(out_ref.dtype)
        else:
            y = y + ybwd_ref[pl.ds(r0, t), :].astype(F32) + dexp_ref[...] * xs
            zz = z_ref[pl.ds(r0, t), :].astype(F32)
            y = y * (zz * (1.0 / (1.0 + jnp.exp(-zz))))
            out_ref[pl.ds(r0, t), :] = _rms(y, gn_ref[...]).astype(out_ref.dtype)
        return carry

    lax.fori_loop(0, n_chunks, chunk_body, 0)


def _ssd_pass(xbc, dt, cw, cb, dtb, alog, *, batch, seq, backward, extra=()):
    n = xbc.shape[0]
    tb = SSD_TILE
    nb = seq // tb
    hb = tb // HALO_ROWS
    n_halo = n // HALO_ROWS
    d = 1 if backward else 0

    def blk_of(b, j):
        return b * nb + ((nb - 1 - j) if backward else j)

    main = lambda w: pl.BlockSpec((tb, w), lambda b, j: (blk_of(b, j), 0))
    full = lambda a: pl.BlockSpec(a.shape, lambda b, j: (0,) * a.ndim)
    in_specs = [
        main(CONV_DIM),
        pl.BlockSpec((HALO_ROWS, CONV_DIM), lambda b, j: (jnp.maximum(blk_of(b, j) * hb - 1, 0), 0)),
        pl.BlockSpec((HALO_ROWS, CONV_DIM), lambda b, j: (jnp.minimum((blk_of(b, j) + 1) * hb, n_halo - 1), 0)),
        pl.BlockSpec((tb, LANES), lambda b, j: (blk_of(b, j), d)),
        full(cw), full(cb), full(dtb), full(alog),
    ]
    args = [xbc, xbc, xbc, dt, cw, cb, dtb, alog]
    if not backward:
        z, ybwd, dexp, gn = extra
        in_specs += [main(SSD_WIDTH), main(SSD_WIDTH), full(dexp), full(gn)]
        args += [z, ybwd, dexp, gn]
    return pl.pallas_call(
        functools.partial(_ssd_kernel, backward=backward, n_blocks=nb),
        grid=(batch, nb),
        in_specs=in_specs,
        out_specs=main(SSD_WIDTH),
        out_shape=jax.ShapeDtypeStruct((n, SSD_WIDTH), BF16),
        scratch_shapes=[pltpu.VMEM((tb + 16, CONV_DIM), F32), pltpu.VMEM((tb, CONV_DIM), F32),
                        pltpu.VMEM((D_STATE, SSD_WIDTH), F32)],
        compiler_params=_params(("parallel", "arbitrary")),
        name="ssd_bwd" if backward else "ssd_fwd",
    )(*args)


def _natten_kernel(q_ref, kp_ref, kc_ref, kn_ref, vp_ref, vc_ref, vn_ref, bias_ref, gn_ref,
                   out_ref, kbuf, vbuf, obuf, *, rows):
    half = NA_ROWS // 2 * GRID_W
    full = NA_ROWS * GRID_W
    jb = pl.program_id(1)
    kbuf[0:half, :] = kp_ref[...]
    kbuf[half:half + full, :] = kc_ref[...]
    kbuf[half + full:, :] = kn_ref[...]
    vbuf[0:half, :] = vp_ref[...]
    vbuf[half:half + full, :] = vc_ref[...]
    vbuf[half + full:, :] = vn_ref[...]
    lane = lax.broadcasted_iota(jnp.int32, (GRID_W, LANES), 1)
    low_half = lane < HEAD_DIM
    n_keys = WIN_H * GRID_W

    def row_body(i, carry):
        r = jb * NA_ROWS + i
        s = jnp.clip(r - WIN_H // 2, 0, rows - WIN_H)
        local = s - (jb * NA_ROWS - NA_ROWS // 2)
        k0 = pl.multiple_of(local * GRID_W, GRID_W)
        q0 = pl.multiple_of(i * GRID_W, GRID_W)
        d0 = s - r + (WIN_H - 1)
        for pp in range(N_HEADS // 2):
            ls = slice(pp * LANES, (pp + 1) * LANES)
            q_pair = q_ref[pl.ds(q0, GRID_W), ls]
            k_win = kbuf[pl.ds(k0, n_keys), ls]
            v_win = vbuf[pl.ds(k0, n_keys), ls]
            zero = jnp.zeros_like(q_pair)
            outs = []
            for hh in range(2):
                h = 2 * pp + hh
                q_h = jnp.where(low_half if hh == 0 else jnp.logical_not(low_half), q_pair, zero)
                sc = _dot_nt(q_h, k_win)
                bias = jnp.concatenate([bias_ref[d0 + 2 * ii, h] for ii in range(WIN_H // 2)], axis=1)
                sc = sc + bias
                m = jnp.max(sc, axis=-1, keepdims=True)
                e = jnp.exp(sc - m)
                denom = jnp.sum(e, axis=-1, keepdims=True)
                o = _dot(e.astype(BF16), v_win)
                outs.append(o * (1.0 / denom))
            obuf[:, ls] = jnp.where(low_half, outs[0], outs[1])
        out_ref[pl.ds(q0, GRID_W), :] = _rms(obuf[...], gn_ref[...]).astype(out_ref.dtype)
        return carry

    lax.fori_loop(0, NA_ROWS, row_body, 0)


def _natten(q, k, v, bias_tab, gn, *, batch, seq):
    n = q.shape[0]
    rows = seq // GRID_W
    nrb = rows // NA_ROWS
    full = NA_ROWS * GRID_W
    half = full // 2
    n_half = n // half
    cur = pl.BlockSpec((full, D_MODEL), lambda b, j: (b * nrb + j, 0))
    prev = pl.BlockSpec((half, D_MODEL), lambda b, j: (jnp.maximum((b * nrb + j) * 2 - 1, 0), 0))
    nxt = pl.BlockSpec((half, D_MODEL), lambda b, j: (jnp.minimum((b * nrb + j) * 2 + 2, n_half - 1), 0))
    const = lambda a: pl.BlockSpec(a.shape, lambda b, j: (0,) * a.ndim)
    return pl.pallas_call(
        functools.partial(_natten_kernel, rows=rows),
        grid=(batch, nrb),
        in_specs=[cur, prev, cur, nxt, prev, cur, nxt, const(bias_tab), const(gn)],
        out_specs=cur,
        out_shape=jax.ShapeDtypeStruct((n, D_MODEL), BF16),
        scratch_shapes=[pltpu.VMEM((2 * full, D_MODEL), BF16), pltpu.VMEM((2 * full, D_MODEL), BF16),
                        pltpu.VMEM((GRID_W, D_MODEL), F32)],
        compiler_params=_params(("parallel", "parallel")),
        name="natten",
    )(q, k, k, k, v, v, v, bias_tab, gn)


def _bias_table(rel_bias):
    qc = np.arange(GRID_W)[:, None]
    kc = np.arange(GRID_W)[None, :]
    s_c = np.clip(qc - WIN_W // 2, 0, GRID_W - WIN_W)
    inside = (kc >= s_c) & (kc < s_c + WIN_W)
    dcol = np.clip(kc - qc, -(WIN_W - 1), WIN_W - 1) + WIN_W - 1
    tab = rel_bias.astype(F32)[:, :, dcol]
    tab = jnp.where(inside[None, None], tab, -jnp.inf)
    pairs = jnp.concatenate([tab[:, :-1], tab[:, 1:]], axis=-1)
    return pairs.transpose(1, 0, 2, 3)


def _outproj_kernel(x_ref, ys_ref, ya_ref, wa_ref, wb_ref, gn_ref, wr_ref, upper_ref,
                    h_ref, u_ref, idx_ref, gate_ref, cnt_ref, carry_ref):
    tm = x_ref.shape[0]

    @pl.when(pl.program_id(0) == 0)
    def _():
        carry_ref[...] = jnp.zeros_like(carry_ref)

    h = x_ref[...] + _dot(ys_ref[...], wa_ref[...]) + _dot(ya_ref[...], wb_ref[...])
    h_ref[...] = h
    u = _rms(h, gn_ref[...])
    u_ref[...] = u

    u_hi = u.astype(BF16)
    u_lo = (u - u_hi.astype(F32)).astype(BF16)
    wr = wr_ref[...]
    w_hi = wr.astype(BF16)
    w_lo = (wr - w_hi.astype(F32)).astype(BF16)
    logit = _dot_nt(w_hi, u_hi) + _dot_nt(w_hi, u_lo) + _dot_nt(w_lo, u_hi)

    lg = logit[0:N_GROUPS]
    gi = lax.broadcasted_iota(jnp.int32, (N_GROUPS, tm), 0)
    g_max = jnp.max(lg, axis=0, keepdims=True)
    g_idx = jnp.min(jnp.where(lg == g_max, gi, N_GROUPS), axis=0, keepdims=True)
    g_val = 1.0 / jnp.sum(jnp.exp(lg - g_max), axis=0, keepdims=True)

    le = logit[8:8 + EXPERTS_PER_GROUP]
    for g in range(1, N_GROUPS):
        le = jnp.where(g_idx == g, logit[8 + g * EXPERTS_PER_GROUP:8 + (g + 1) * EXPERTS_PER_GROUP], le)
    ei = lax.broadcasted_iota(jnp.int32, (EXPERTS_PER_GROUP, tm), 0)
    m1 = jnp.max(le, axis=0, keepdims=True)
    i1 = jnp.min(jnp.where(le == m1, ei, EXPERTS_PER_GROUP), axis=0, keepdims=True)
    le2 = jnp.where(ei == i1, -jnp.inf, le)
    m2 = jnp.max(le2, axis=0, keepdims=True)
    i2 = jnp.min(jnp.where(le2 == m2, ei, EXPERTS_PER_GROUP), axis=0, keepdims=True)
    e21 = jnp.exp(m2 - m1)
    den = 1.0 / (1.0 + e21)
    gate1 = g_val * den
    gate2 = g_val * e21 * den
    e1 = g_idx * EXPERTS_PER_GROUP + i1
    e2 = g_idx * EXPERTS_PER_GROUP + i2

    xi = lax.broadcasted_iota(jnp.int32, (N_EXPERTS, tm), 0)
    hot1 = xi == e1
    hot2 = xi == e2
    hot = jnp.where(jnp.logical_or(hot1, hot2), 1.0, 0.0)
    before = _dot(hot.astype(BF16), upper_ref[...])
    carry = carry_ref[...]
    before = before + jnp.tile(carry, (1, tm // LANES))
    rank1 = jnp.sum(jnp.where(hot1, before, 0.0), axis=0, keepdims=True)
    rank2 = jnp.sum(jnp.where(hot2, before, 0.0), axis=0, keepdims=True)
    carry = carry + jnp.sum(hot, axis=1, keepdims=True)
    carry_ref[...] = carry
    cnt_ref[...] = carry

    zi = jnp.zeros((4, tm), jnp.int32)
    idx_ref[...] = jnp.concatenate([e1, e2, rank1.astype(jnp.int32), rank2.astype(jnp.int32), zi], axis=0)
    gates = jnp.concatenate([gate1, gate2, jnp.zeros((LANES - 2, tm), F32)], axis=0)
    gate_ref[...] = gates.T


def _outproj(x, y_ssd, y_att, wa, wb, gn, wr, upper):
    n = x.shape[0]
    tm = TOKEN_TILE
    row = lambda w: pl.BlockSpec((tm, w), lambda i: (i, 0))
    full = lambda a: pl.BlockSpec(a.shape, lambda i: (0,) * a.ndim)
    return pl.pallas_call(
        _outproj_kernel,
        grid=(n // tm,),
        in_specs=[row(D_MODEL), row(D_MODEL), row(D_MODEL), full(wa), full(wb), full(gn), full(wr), full(upper)],
        out_specs=[row(D_MODEL), row(D_MODEL), pl.BlockSpec((8, tm), lambda i: (0, i)), row(LANES),
                   pl.BlockSpec((N_EXPERTS, LANES), lambda i: (0, 0))],
        out_shape=[jax.ShapeDtypeStruct((n, D_MODEL), F32), jax.ShapeDtypeStruct((n, D_MODEL), F32),
                   jax.ShapeDtypeStruct((8, n), jnp.int32), jax.ShapeDtypeStruct((n, LANES), F32),
                   jax.ShapeDtypeStruct((N_EXPERTS, LANES), F32)],
        scratch_shapes=[pltpu.VMEM((N_EXPERTS, LANES), F32)],
        compiler_params=_params(("arbitrary",)),
        name="outproj_router",
    )(x, y_ssd, y_att, wa, wb, gn, wr, upper)


def _row_copy(src_ref, src_row, dst_ref, dst_row, sem):
    return pltpu.make_async_copy(src_ref.at[pl.ds(src_row, 1)], dst_ref.at[pl.ds(dst_row, 1)], sem)


def _dispatch_kernel(dest_ref, u_ref, slots_in_ref, slots_ref, sem):
    del slots_in_ref
    tm = u_ref.shape[0]

    def start(r, carry):
        for kk in range(2):
            _row_copy(u_ref, r, slots_ref, dest_ref[0, 0, kk * tm + r], sem).start()
        return carry

    def wait(r, carry):
        for kk in range(2):
            _row_copy(u_ref, r, slots_ref, dest_ref[0, 0, kk * tm + r], sem).wait()
        return carry

    lax.fori_loop(0, tm, start, 0)
    lax.fori_loop(0, tm, wait, 0)


def _dispatch(u, dest_tiles, n_slots):
    n = u.shape[0]
    tm = dest_tiles.shape[2] // 2
    slots0 = jnp.zeros((n_slots, D_MODEL), F32)
    return pl.pallas_call(
        _dispatch_kernel,
        grid=(n // tm,),
        in_specs=[pl.BlockSpec((1, 1, 2 * tm), lambda i: (i, 0, 0), memory_space=pltpu.SMEM),
                  pl.BlockSpec((tm, D_MODEL), lambda i: (i, 0)),
                  pl.BlockSpec(memory_space=pl.ANY)],
        out_specs=pl.BlockSpec(memory_space=pl.ANY),
        out_shape=jax.ShapeDtypeStruct((n_slots, D_MODEL), F32),
        scratch_shapes=[pltpu.SemaphoreType.DMA(())],
        input_output_aliases={2: 0},
        compiler_params=_params(("arbitrary",)),
        name="dispatch",
    )(dest_tiles, u, slots0)


def _expert_kernel(blk_e_ref, used_ref, x_ref, wg_ref, wu_ref, wd_ref, y_ref):
    del blk_e_ref

    @pl.when(pl.program_id(0) < used_ref[0])
    def _():
        x = x_ref[...].astype(BF16)
        a = _dot(x, wg_ref[0])
        b = _dot(x, wu_ref[0])
        hid = (a * (1.0 / (1.0 + jnp.exp(-a))) * b).astype(BF16)
        y_ref[...] = _dot(hid, wd_ref[0])

    @pl.when(pl.program_id(0) >= used_ref[0])
    def _():
        y_ref[...] = jnp.zeros_like(y_ref)


def _experts(slots, blk_e, used, wg, wu, wd):
    n_slots = slots.shape[0]
    nb = n_slots // SLOT_BLOCK
    xmap = lambda i, be, us: (jnp.minimum(i, us[0] - 1), 0)
    wmap = lambda i, be, us: (be[i], 0, 0)
    grid_spec = pltpu.PrefetchScalarGridSpec(
        num_scalar_prefetch=2,
        grid=(nb,),
        in_specs=[pl.BlockSpec((SLOT_BLOCK, D_MODEL), xmap),
                  pl.BlockSpec((1, D_MODEL, D_EXPERT), wmap),
                  pl.BlockSpec((1, D_MODEL, D_EXPERT), wmap),
                  pl.BlockSpec((1, D_EXPERT, D_MODEL), wmap)],
        out_specs=pl.BlockSpec((SLOT_BLOCK, D_MODEL), lambda i, be, us: (i, 0)),
    )
    return pl.pallas_call(
        _expert_kernel,
        grid_spec=grid_spec,
        out_shape=jax.ShapeDtypeStruct((n_slots, D_MODEL), F32),
        compiler_params=_params(("arbitrary",)),
        name="experts",
    )(blk_e, used, slots, wg, wu, wd)


def _combine_kernel(dest_ref, h_ref, gate_ref, p_ref, wp_ref, wg_ref, gp_ref, gf_ref, y_ref,
                    out_ref, buf, sem):
    tm = h_ref.shape[0]

    def start(r, carry):
        for kk in range(2):
            _row_copy(y_ref, dest_ref[0, 0, kk * tm + r], buf.at[kk], r, sem).start()
        return carry

    def wait(r, carry):
        for kk in range(2):
            _row_copy(y_ref, dest_ref[0, 0, kk * tm + r], buf.at[kk], r, sem).wait()
        return carry

    lax.fori_loop(0, tm, start, 0)
    lax.fori_loop(0, tm, wait, 0)

    gates = gate_ref[...]
    moe = gates[:, 0:1] * buf[0] + gates[:, 1:2] * buf[1]
    h = h_ref[...] + moe
    u = _rms(h, gp_ref[...]).astype(BF16)
    gate = 1.0 / (1.0 + jnp.exp(-_dot(u, wg_ref[...])))
    h = h + _dot(p_ref[...].astype(BF16), wp_ref[...]) * gate
    out_ref[...] = _rms(h, gf_ref[...])


def _combine(dest_tiles, h, gate_col, p, wp, wg, gp, gf, y_sorted):
    n = h.shape[0]
    tm = dest_tiles.shape[2] // 2
    row = lambda w: pl.BlockSpec((tm, w), lambda i: (i, 0))
    full = lambda a: pl.BlockSpec(a.shape, lambda i: (0,) * a.ndim)
    return pl.pallas_call(
        _combine_kernel,
        grid=(n // tm,),
        in_specs=[pl.BlockSpec((1, 1, 2 * tm), lambda i: (i, 0, 0), memory_space=pltpu.SMEM),
                  row(D_MODEL), row(LANES), row(PLE_DIM), full(wp), full(wg), full(gp), full(gf),
                  pl.BlockSpec(memory_space=pl.ANY)],
        out_specs=row(D_MODEL),
        out_shape=jax.ShapeDtypeStruct((n, D_MODEL), F32),
        scratch_shapes=[pltpu.VMEM((2, tm, D_MODEL), F32), pltpu.SemaphoreType.DMA(())],
        compiler_params=_params(("arbitrary",)),
        name="combine_ple",
    )(dest_tiles, h, gate_col, p, wp, wg, gp, gf, y_sorted)


def _pad_lanes(v, width=LANES):
    return jnp.zeros((1, width), F32).at[0, :v.shape[0]].set(v.astype(F32))


def _encoder(x3, p3, prm):
    batch, seq, _ = x3.shape
    n = batch * seq
    assert seq % SSD_TILE == 0 and (seq // GRID_W) % NA_ROWS == 0 and n % TOKEN_TILE == 0
    x = x3.reshape(n, D_MODEL)
    p = p3.reshape(n, PLE_DIM)

    z, xbc, q, k, v, dt = _inproj(x, prm["norm_mix"], prm["w_z"], prm["w_xbc"], prm["w_q"], prm["w_k"],
                                  prm["w_v"], prm["w_dt"])

    ssd_args = (xbc, dt, prm["conv_w"], prm["conv_b"])
    y_bwd = _ssd_pass(*ssd_args, prm["dt_bias"][1], prm["a_log"][1], batch=batch, seq=seq, backward=True)
    y_ssd = _ssd_pass(*ssd_args, prm["dt_bias"][0], prm["a_log"][0], batch=batch, seq=seq, backward=False,
                      extra=(z, y_bwd, prm["d_exp"], prm["ssd_norm"]))
    y_att = _natten(q, k, v, prm["bias_tab"], prm["attn_norm"], batch=batch, seq=seq)

    h1, u2, ridx, gate_col, counts = _outproj(x, y_ssd, y_att, prm["w_out_a"], prm["w_out_b"], prm["norm_ffn"],
                                              prm["w_router"], prm["upper"])

    cnt = counts[:, 0].astype(jnp.int32)
    padded = (cnt + SLOT_BLOCK - 1) // SLOT_BLOCK * SLOT_BLOCK
    pad_end = jnp.cumsum(padded)
    pad_start = pad_end - padded
    n_blocks = -(-(2 * n + N_EXPERTS * (SLOT_BLOCK - 1)) // SLOT_BLOCK)
    n_slots = n_blocks * SLOT_BLOCK
    dest = pad_start[ridx[0:2]] + ridx[2:4]
    tm = TOKEN_TILE // 2
    dest_tiles = dest.reshape(2, n // tm, tm).transpose(1, 0, 2).reshape(n // tm, 1, 2 * tm)
    blk_e = jnp.minimum(jnp.searchsorted(pad_end, jnp.arange(n_blocks, dtype=jnp.int32) * SLOT_BLOCK, side="right"),
                        N_EXPERTS - 1).astype(jnp.int32)
    used = (pad_end[-1:] // SLOT_BLOCK).astype(jnp.int32)

    slots = _dispatch(u2, dest_tiles, n_slots)
    y_sorted = _experts(slots, blk_e, used, prm["w_gate"], prm["w_up"], prm["w_down"])
    out = _combine(dest_tiles, h1, gate_col, p, prm["w_ple_proj"], prm["w_ple_gate"], prm["norm_ple"],
                   prm["norm_final"], y_sorted)
    return out.reshape(batch, seq, D_MODEL)


def _prepare(w_in, conv_w, conv_b, dt_bias, a_log, d_skip, ssd_norm, rel_bias, attn_norm, w_out, norm_mix,
             norm_ffn, router_group, router_expert, w_gate, w_up, w_down, norm_ple, w_ple_proj, w_ple_gate,
             norm_final):
    w_in = w_in[0]
    offs = np.cumsum([0, SSD_WIDTH, CONV_DIM, 2 * N_HEADS, D_MODEL, D_MODEL, D_MODEL])
    piece = lambda i: w_in[:, offs[i]:offs[i + 1]]
    w_dt = jnp.zeros((D_MODEL, 2 * LANES), F32)
    w_dt = w_dt.at[:, :N_HEADS].set(piece(2)[:, :N_HEADS]).at[:, LANES:LANES + N_HEADS].set(piece(2)[:, N_HEADS:])
    row = lambda v: v.astype(F32).reshape(1, -1)
    w_router = jnp.zeros((ROUTER_ROWS, D_MODEL), F32)
    w_router = w_router.at[:N_GROUPS].set(router_group[0].T)
    w_router = w_router.at[8:8 + N_EXPERTS].set(router_expert[0].transpose(0, 2, 1).reshape(N_EXPERTS, D_MODEL))
    tm = TOKEN_TILE
    upper = jnp.asarray(np.triu(np.ones((tm, tm), np.float32), k=1), BF16)
    return {
        "norm_mix": row(norm_mix[0]),
        "w_z": piece(0).astype(BF16), "w_xbc": piece(1).astype(BF16), "w_dt": w_dt.astype(BF16),
        "w_q": piece(3).astype(BF16), "w_k": piece(4).astype(BF16), "w_v": piece(5).astype(BF16),
        "conv_w": jnp.zeros((8, CONV_DIM), F32).at[:D_CONV].set(conv_w[0]),
        "conv_b": row(conv_b[0]),
        "dt_bias": [_pad_lanes(dt_bias[0, d]) for d in range(2)],
        "a_log": [jnp.full((1, LANES), -jnp.inf, F32).at[0, :N_HEADS].set(a_log[0, d]) for d in range(2)],
        "d_exp": jnp.repeat(d_skip[0].astype(F32), HEAD_DIM).reshape(1, SSD_WIDTH),
        "ssd_norm": row(ssd_norm[0]),
        "bias_tab": _bias_table(rel_bias[0]),
        "attn_norm": row(attn_norm[0]),
        "w_out_a": w_out[0, :SSD_WIDTH].astype(BF16), "w_out_b": w_out[0, SSD_WIDTH:].astype(BF16),
        "norm_ffn": row(norm_ffn[0]),
        "w_router": w_router,
        "upper": upper,
        "w_gate": w_gate[0].astype(BF16), "w_up": w_up[0].astype(BF16), "w_down": w_down[0].astype(BF16),
        "norm_ple": row(norm_ple[0]),
        "w_ple_proj": w_ple_proj[0].astype(BF16), "w_ple_gate": w_ple_gate[0].astype(BF16),
        "norm_final": row(norm_final),
    }


def kernel(x_prompt, x_sample, p_prompt, p_sample, w_in, conv_w, conv_b, dt_bias, a_log, d_skip, ssd_norm,
           rel_bias, attn_norm, w_out, norm_mix, norm_ffn, router_group, router_expert, w_gate, w_up, w_down,
           norm_ple, w_ple_proj, w_ple_gate, norm_final):
    prm = _prepare(w_in, conv_w, conv_b, dt_bias, a_log, d_skip, ssd_norm, rel_bias, attn_norm, w_out, norm_mix,
                   norm_ffn, router_group, router_expert, w_gate, w_up, w_down, norm_ple, w_ple_proj, w_ple_gate,
                   norm_final)
    y_prompt = _encoder(x_prompt, p_prompt[0], prm)
    y_sample = _encoder(x_sample, p_sample[0], prm)
    return (y_prompt, y_sample)
```

```python
import functools
import math

import numpy as np
import jax
import jax.numpy as jnp
from jax import lax
from jax.experimental import pallas as pl
from jax.experimental.pallas import tpu as pltpu

F32 = jnp.float32
BF16 = jnp.bfloat16

D_MODEL = 1024
SSD_WIDTH = 1024
HEAD_DIM = 64
N_HEADS = 16
N_BC_GROUPS = 2
D_STATE = 128
D_CONV = 5
CHUNK = 128
CONV_DIM = SSD_WIDTH + 2 * N_BC_GROUPS * D_STATE
GRID_W = 64
WIN_H = 8
WIN_W = 16
N_GROUPS = 4
EXPERTS_PER_GROUP = 8
N_EXPERTS = N_GROUPS * EXPERTS_PER_GROUP
D_EXPERT = 512
PLE_DIM = 256
EPS = 1e-6
LOG2E = math.log2(math.e)

LANES = 128
HALO_ROWS = 16
TOKEN_TILE = 512
SSD_TILE = 512
NA_ROWS = 8
SLOT_BLOCK = 512
ROUTE_TILE = 256
ROW_GROUP = 8
ROW_SUB = 8
ROUTER_ROWS = 48
VMEM_LIMIT = 56 * 1024 * 1024


def _rms(x, g):
    ms = jnp.mean(x * x, axis=-1, keepdims=True)
    return x * lax.rsqrt(ms + EPS) * g


def _split3(x):
    hi = x.astype(BF16)
    r1 = x - hi.astype(F32)
    mid = r1.astype(BF16)
    lo = (r1 - mid.astype(F32)).astype(BF16)
    return hi, mid, lo


def _dot(a, b):
    return jnp.dot(a, b, preferred_element_type=F32)


def _dot_nt(a, b):
    return lax.dot_general(a, b, (((1,), (1,)), ((), ())), preferred_element_type=F32)


def _tiled(m):
    return (m * ROW_SUB, LANES)


def _store_row_tiles(ref, x):
    m = x.shape[0]
    for s in range(ROW_SUB):
        ref[pl.ds(s, m, stride=ROW_SUB), :] = x[:, s * LANES:(s + 1) * LANES]


def _load_row_tiles(ref):
    m = ref.shape[0] // ROW_SUB
    return jnp.concatenate([ref[pl.ds(s, m, stride=ROW_SUB), :] for s in range(ROW_SUB)], axis=1)


def _params(sem):
    return pltpu.CompilerParams(dimension_semantics=sem, vmem_limit_bytes=VMEM_LIMIT)


def _inproj_kernel(x_ref, g_ref, wz_ref, wxbc_ref, wq_ref, wk_ref, wv_ref, wdt_ref,
                   z_ref, xbc_ref, q_ref, k_ref, v_ref, dt_ref):
    u = _rms(x_ref[...], g_ref[...]).astype(BF16)
    z_ref[...] = _dot(u, wz_ref[...]).astype(BF16)
    xbc_ref[...] = _dot(u, wxbc_ref[...]).astype(BF16)
    q_ref[...] = (_dot(u, wq_ref[...]) * (HEAD_DIM ** -0.5 * LOG2E)).astype(BF16)
    k_ref[...] = _dot(u, wk_ref[...]).astype(BF16)
    v_ref[...] = _dot(u, wv_ref[...]).astype(BF16)
    dt_ref[...] = _dot(u, wdt_ref[...])


def _inproj(x, g, wz, wxbc, wq, wk, wv, wdt):
    n = x.shape[0]
    tm = TOKEN_TILE
    row = lambda w: pl.BlockSpec((tm, w), lambda i: (i, 0))
    full = lambda a: pl.BlockSpec(a.shape, lambda i: (0,) * a.ndim)
    outs = [(D_MODEL, BF16), (CONV_DIM, BF16), (D_MODEL, BF16), (D_MODEL, BF16), (D_MODEL, BF16),
            (2 * LANES, F32)]
    return pl.pallas_call(
        _inproj_kernel,
        grid=(n // tm,),
        in_specs=[row(D_MODEL), full(g), full(wz), full(wxbc), full(wq), full(wk), full(wv), full(wdt)],
        out_specs=[row(w) for w, _ in outs],
        out_shape=[jax.ShapeDtypeStruct((n, w), d) for w, d in outs],
        compiler_params=_params(("parallel",)),
        name="inproj",
    )(x, g, wz, wxbc, wq, wk, wv, wdt)


def _ssd_kernel(*refs, backward, n_blocks):
    if backward:
        (xbc_ref, prev_ref, next_ref, dt_ref, cw_ref, cb_ref, dtb_ref, alog_ref,
         out_ref, act_ref, xp_ref, st_ref) = refs
    else:
        (act_ref, dt_ref, dtb_ref, alog_ref, z_ref, ybwd_ref, dexp_ref, gn_ref, out_ref, st_ref) = refs
    tb = act_ref.shape[0]
    t = CHUNK
    n_chunks = tb // t
    j = pl.program_id(1)

    @pl.when(j == 0)
    def _():
        st_ref[...] = jnp.zeros_like(st_ref)

    if backward:
        blk = n_blocks - 1 - j
        prev_rows = prev_ref[HALO_ROWS - 8:, :].astype(F32)
        next_rows = next_ref[:8, :].astype(F32)
        xp_ref[0:8, :] = jnp.where(blk == 0, 0.0, prev_rows)
        xp_ref[8:8 + tb, :] = xbc_ref[...].astype(F32)
        xp_ref[8 + tb:, :] = jnp.where(blk == n_blocks - 1, 0.0, next_rows)
        for c in range(n_chunks):
            acc = cb_ref[...] + cw_ref[0:1, :] * xp_ref[c * t + 6:c * t + 6 + t, :]
            for kk in range(1, D_CONV):
                acc = acc + cw_ref[kk:kk + 1, :] * xp_ref[c * t + 6 + kk:c * t + 6 + kk + t, :]
            act_ref[c * t:(c + 1) * t, :] = (acc * (1.0 / (1.0 + jnp.exp(-acc)))).astype(act_ref.dtype)

    a_row = -jnp.exp(alog_ref[...])
    dtb = dtb_ref[...]
    ri = lax.broadcasted_iota(jnp.int32, (t, t), 0)
    ci = lax.broadcasted_iota(jnp.int32, (t, t), 1)
    if backward:
        keep = ci >= ri
    else:
        keep = ri >= ci
    tri = jnp.where(keep, 1.0, 0.0).astype(BF16)
    er = lax.broadcasted_iota(jnp.int32, (2 * LANES, SSD_WIDTH), 0)
    ec = lax.broadcasted_iota(jnp.int32, (2 * LANES, SSD_WIDTH), 1)
    expand = jnp.where((er % LANES) == (ec // HEAD_DIM), 1.0, 0.0).astype(BF16)
    lane = lax.broadcasted_iota(jnp.int32, (t, LANES), 1)
    low_half = lane < HEAD_DIM

    def chunk_body(i, carry):
        c = (n_chunks - 1 - i) if backward else i
        r0 = pl.multiple_of(c * t, t)
        act = act_ref[pl.ds(r0, t), :].astype(F32)
        xs = act[:, :SSD_WIDTH]
        xs_bf = xs.astype(BF16)
        bm = act[:, SSD_WIDTH:SSD_WIDTH + N_BC_GROUPS * D_STATE]
        cm = act[:, SSD_WIDTH + N_BC_GROUPS * D_STATE:]

        dt_raw = dt_ref[pl.ds(r0, t), :] + dtb
        dt = jnp.maximum(dt_raw, 0.0) + jnp.log(1.0 + jnp.exp(-jnp.abs(dt_raw)))
        da = dt * a_row
        hi, mid, lo = _split3(da)
        cs = _dot(tri, hi) + _dot(tri, mid) + _dot(tri, lo)
        cs_row = cs.T
        dt_row = dt.T
        edge = cs[0:1, :] if backward else cs[t - 1:t, :]
        w_in = dt * jnp.exp(edge - cs)
        ecs = jnp.exp(cs)
        both = jnp.concatenate([w_in, ecs], axis=0)
        b_hi = both.astype(BF16)
        b_lo = (both - b_hi.astype(F32)).astype(BF16)
        both_x = _dot(jnp.concatenate([b_hi, b_lo], axis=1), expand)
        w_x = both_x[:t]
        ecs_x = both_x[t:]
        cd_x = ecs_x[0:1, :] if backward else ecs_x[t - 1:t, :]

        st = st_ref[...]
        st_bf = st.astype(BF16)
        xw = (xs * w_x).astype(BF16)
        y_parts = []
        new_parts = []
        for g in range(N_BC_GROUPS):
            gs = slice(g * D_STATE, (g + 1) * D_STATE)
            hs = slice(g * (SSD_WIDTH // N_BC_GROUPS), (g + 1) * (SSD_WIDTH // N_BC_GROUPS))
            b_g = bm[:, gs]
            c_g = cm[:, gs].astype(BF16)
            cb = _dot_nt(c_g, b_g.astype(BF16))
            y_off = _dot(c_g, st_bf[:, hs])
            new_parts.append(_dot(b_g.T.astype(BF16), xw[:, hs]))
            heads_per_group = N_HEADS // N_BC_GROUPS
            pair_parts = []
            for pp in range(heads_per_group // 2):
                h0 = g * heads_per_group + 2 * pp
                xs_pair = xs_bf[:, h0 * HEAD_DIM:(h0 + 2) * HEAD_DIM]
                ys = []
                for h in (h0, h0 + 1):
                    diff = cs[:, h:h + 1] - cs_row[h:h + 1, :]
                    seg = jnp.exp(jnp.where(keep, diff, -jnp.inf))
                    m = (cb * seg * dt_row[h:h + 1, :]).astype(BF16)
                    ys.append(_dot(m, xs_pair))
                pair_parts.append(jnp.where(low_half, ys[0], ys[1]))
            y_diag = jnp.concatenate(pair_parts, axis=1)
            y_parts.append(y_diag + y_off * ecs_x[:, hs])
        y = jnp.concatenate(y_parts, axis=1)
        st_ref[...] = st * cd_x + jnp.concatenate(new_parts, axis=1)

        if backward:
            out_ref[pl.ds(r0, t), :] = y.astype(out_ref.dtype)
        else:
            y = y + ybwd_ref[pl.ds(r0, t), :].astype(F32) + dexp_ref[...] * xs
            zz = z_ref[pl.ds(r0, t), :].astype(F32)
            y = y * (zz * (1.0 / (1.0 + jnp.exp(-zz))))
            out_ref[pl.ds(r0, t), :] = _rms(y, gn_ref[...]).astype(out_ref.dtype)
        return carry

    lax.fori_loop(0, n_chunks, chunk_body, 0)


def _ssd_pass(xin, dt, dtb, alog, *, batch, seq, backward, extra=()):
    n = xin.shape[0]
    tb = SSD_TILE
    nb = seq // tb
    hb = tb // HALO_ROWS
    n_halo = n // HALO_ROWS
    d = 1 if backward else 0

    def blk_of(b, j):
        return b * nb + ((nb - 1 - j) if backward else j)

    main = lambda w: pl.BlockSpec((tb, w), lambda b, j: (blk_of(b, j), 0))
    full = lambda a: pl.BlockSpec(a.shape, lambda b, j: (0,) * a.ndim)
    dt_spec = pl.BlockSpec((tb, LANES), lambda b, j: (blk_of(b, j), d))
    state = pltpu.VMEM((D_STATE, SSD_WIDTH), F32)
    y_shape = jax.ShapeDtypeStruct((n, SSD_WIDTH), BF16)
    if backward:
        cw, cb = extra
        in_specs = [
            main(CONV_DIM),
            pl.BlockSpec((HALO_ROWS, CONV_DIM), lambda b, j: (jnp.maximum(blk_of(b, j) * hb - 1, 0), 0)),
            pl.BlockSpec((HALO_ROWS, CONV_DIM), lambda b, j: (jnp.minimum((blk_of(b, j) + 1) * hb, n_halo - 1), 0)),
            dt_spec, full(cw), full(cb), full(dtb), full(alog),
        ]
        args = [xin, xin, xin, dt, cw, cb, dtb, alog]
        out_specs = [main(SSD_WIDTH), main(CONV_DIM)]
        out_shape = [y_shape, jax.ShapeDtypeStruct((n, CONV_DIM), BF16)]
        scratch = [pltpu.VMEM((tb + 16, CONV_DIM), F32), state]
    else:
        z, ybwd, dexp, gn = extra
        in_specs = [main(CONV_DIM), dt_spec, full(dtb), full(alog), main(SSD_WIDTH), main(SSD_WIDTH), full(dexp),
                    full(gn)]
        args = [xin, dt, dtb, alog, z, ybwd, dexp, gn]
        out_specs = main(SSD_WIDTH)
        out_shape = y_shape
        scratch = [state]
    return pl.pallas_call(
        functools.partial(_ssd_kernel, backward=backward, n_blocks=nb),
        grid=(batch, nb),
        in_specs=in_specs,
        out_specs=out_specs,
        out_shape=out_shape,
        scratch_shapes=scratch,
        compiler_params=_params(("parallel", "arbitrary")),
        name="ssd_bwd" if backward else "ssd_fwd",
    )(*args)


def _natten_kernel(q_ref, kp_ref, kc_ref, kn_ref, vp_ref, vc_ref, vn_ref, bias_ref, gn_ref,
                   out_ref, kbuf, vbuf, obuf, *, rows):
    half = NA_ROWS // 2 * GRID_W
    full = NA_ROWS * GRID_W
    jb = pl.program_id(1)
    kbuf[0:half, :] = kp_ref[...]
    kbuf[half:half + full, :] = kc_ref[...]
    kbuf[half + full:, :] = kn_ref[...]
    vbuf[0:half, :] = vp_ref[...]
    vbuf[half:half + full, :] = vc_ref[...]
    vbuf[half + full:, :] = vn_ref[...]
    lane = lax.broadcasted_iota(jnp.int32, (GRID_W, LANES), 1)
    low_half = lane < HEAD_DIM
    n_keys = WIN_H * GRID_W

    def row_body(i, carry):
        r = jb * NA_ROWS + i
        s = jnp.clip(r - WIN_H // 2, 0, rows - WIN_H)
        local = s - (jb * NA_ROWS - NA_ROWS // 2)
        k0 = pl.multiple_of(local * GRID_W, GRID_W)
        q0 = pl.multiple_of(i * GRID_W, GRID_W)
        d0 = s - r + (WIN_H - 1)
        n_pairs = N_HEADS // 2
        lanes_of = lambda pp: slice(pp * LANES, (pp + 1) * LANES)
        scores = []
        for pp in range(n_pairs):
            q_pair = q_ref[pl.ds(q0, GRID_W), lanes_of(pp)]
            zero = jnp.zeros_like(q_pair)
            q2 = jnp.concatenate([jnp.where(low_half, q_pair, zero), jnp.where(low_half, zero, q_pair)], axis=0)
            sc = _dot_nt(q2, kbuf[pl.ds(k0, n_keys), lanes_of(pp)])
            bias = jnp.concatenate(
                [jnp.concatenate([bias_ref[d0 + 2 * ii, 2 * pp + hh] for ii in range(WIN_H // 2)], axis=1)
                 for hh in range(2)], axis=0)
            scores.append(sc + bias)
        probs = []
        for pp in range(n_pairs):
            m = jnp.max(scores[pp], axis=-1, keepdims=True)
            e = jnp.exp2(scores[pp] - m)
            probs.append((e.astype(BF16), 1.0 / jnp.sum(e, axis=-1, keepdims=True)))
        for pp in range(n_pairs):
            e, inv = probs[pp]
            o2 = _dot(e, vbuf[pl.ds(k0, n_keys), lanes_of(pp)]) * inv
            obuf[:, lanes_of(pp)] = jnp.where(low_half, o2[:GRID_W], o2[GRID_W:])
        out_ref[pl.ds(q0, GRID_W), :] = _rms(obuf[...], gn_ref[...]).astype(out_ref.dtype)
        return carry

    lax.fori_loop(0, NA_ROWS, row_body, 0)


def _natten(q, k, v, bias_tab, gn, *, batch, seq):
    n = q.shape[0]
    rows = seq // GRID_W
    nrb = rows // NA_ROWS
    full = NA_ROWS * GRID_W
    half = full // 2
    n_half = n // half
    cur = pl.BlockSpec((full, D_MODEL), lambda b, j: (b * nrb + j, 0))
    prev = pl.BlockSpec((half, D_MODEL), lambda b, j: (jnp.maximum((b * nrb + j) * 2 - 1, 0), 0))
    nxt = pl.BlockSpec((half, D_MODEL), lambda b, j: (jnp.minimum((b * nrb + j) * 2 + 2, n_half - 1), 0))
    const = lambda a: pl.BlockSpec(a.shape, lambda b, j: (0,) * a.ndim)
    return pl.pallas_call(
        functools.partial(_natten_kernel, rows=rows),
        grid=(batch, nrb),
        in_specs=[cur, prev, cur, nxt, prev, cur, nxt, const(bias_tab), const(gn)],
        out_specs=cur,
        out_shape=jax.ShapeDtypeStruct((n, D_MODEL), BF16),
        scratch_shapes=[pltpu.VMEM((2 * full, D_MODEL), BF16), pltpu.VMEM((2 * full, D_MODEL), BF16),
                        pltpu.VMEM((GRID_W, D_MODEL), F32)],
        compiler_params=_params(("parallel", "parallel")),
        name="natten",
    )(q, k, k, k, v, v, v, bias_tab, gn)


def _bias_table(rel_bias):
    qc = np.arange(GRID_W)[:, None]
    kc = np.arange(GRID_W)[None, :]
    s_c = np.clip(qc - WIN_W // 2, 0, GRID_W - WIN_W)
    inside = (kc >= s_c) & (kc < s_c + WIN_W)
    dcol = np.clip(kc - qc, -(WIN_W - 1), WIN_W - 1) + WIN_W - 1
    tab = rel_bias.astype(F32)[:, :, dcol]
    tab = jnp.where(inside[None, None], tab * LOG2E, -jnp.inf)
    pairs = jnp.concatenate([tab[:, :-1], tab[:, 1:]], axis=-1)
    return pairs.transpose(1, 0, 2, 3)


def _outproj_kernel(x_ref, ys_ref, ya_ref, wa_ref, wb_ref, gn_ref, wr_ref, upper_ref,
                    h_ref, u_ref, idx_ref, gate_ref, cnt_ref, carry_ref):
    tm = x_ref.shape[0]

    @pl.when(pl.program_id(0) == 0)
    def _():
        carry_ref[...] = jnp.zeros_like(carry_ref)

    h = x_ref[...] + _dot(ys_ref[...], wa_ref[...]) + _dot(ya_ref[...], wb_ref[...])
    h_ref[...] = h
    u = _rms(h, gn_ref[...])
    _store_row_tiles(u_ref, u)

    u_hi = u.astype(BF16)
    u_lo = (u - u_hi.astype(F32)).astype(BF16)
    wr = wr_ref[...]
    w_hi = wr.astype(BF16)
    w_lo = (wr - w_hi.astype(F32)).astype(BF16)
    logit = _dot_nt(w_hi, u_hi) + _dot_nt(w_hi, u_lo) + _dot_nt(w_lo, u_hi)

    lg = logit[0:N_GROUPS]
    gi = lax.broadcasted_iota(jnp.int32, (N_GROUPS, tm), 0)
    g_max = jnp.max(lg, axis=0, keepdims=True)
    g_idx = jnp.min(jnp.where(lg == g_max, gi, N_GROUPS), axis=0, keepdims=True)
    g_val = 1.0 / jnp.sum(jnp.exp(lg - g_max), axis=0, keepdims=True)

    le = logit[8:8 + EXPERTS_PER_GROUP]
    for g in range(1, N_GROUPS):
        le = jnp.where(g_idx == g, logit[8 + g * EXPERTS_PER_GROUP:8 + (g + 1) * EXPERTS_PER_GROUP], le)
    ei = lax.broadcasted_iota(jnp.int32, (EXPERTS_PER_GROUP, tm), 0)
    m1 = jnp.max(le, axis=0, keepdims=True)
    i1 = jnp.min(jnp.where(le == m1, ei, EXPERTS_PER_GROUP), axis=0, keepdims=True)
    le2 = jnp.where(ei == i1, -jnp.inf, le)
    m2 = jnp.max(le2, axis=0, keepdims=True)
    i2 = jnp.min(jnp.where(le2 == m2, ei, EXPERTS_PER_GROUP), axis=0, keepdims=True)
    e21 = jnp.exp(m2 - m1)
    den = 1.0 / (1.0 + e21)
    gate1 = g_val * den
    gate2 = g_val * e21 * den
    e1 = g_idx * EXPERTS_PER_GROUP + i1
    e2 = g_idx * EXPERTS_PER_GROUP + i2

    xi = lax.broadcasted_iota(jnp.int32, (N_EXPERTS, tm), 0)
    hot1 = xi == e1
    hot2 = xi == e2
    hot = jnp.where(jnp.logical_or(hot1, hot2), 1.0, 0.0)
    before = _dot(hot.astype(BF16), upper_ref[...])
    carry = carry_ref[...]
    before = before + jnp.tile(carry, (1, tm // LANES))
    rank1 = jnp.sum(jnp.where(hot1, before, 0.0), axis=0, keepdims=True)
    rank2 = jnp.sum(jnp.where(hot2, before, 0.0), axis=0, keepdims=True)
    carry = carry + jnp.sum(hot, axis=1, keepdims=True)
    carry_ref[...] = carry
    cnt_ref[...] = carry

    zi = jnp.zeros((4, tm), jnp.int32)
    idx_ref[...] = jnp.concatenate([e1, e2, rank1.astype(jnp.int32), rank2.astype(jnp.int32), zi], axis=0)
    gates = jnp.concatenate([gate1, gate2, jnp.zeros((LANES - 2, tm), F32)], axis=0)
    gate_ref[...] = gates.T


def _outproj(x, y_ssd, y_att, wa, wb, gn, wr, upper):
    n = x.shape[0]
    tm = TOKEN_TILE
    row = lambda w: pl.BlockSpec((tm, w), lambda i: (i, 0))
    full = lambda a: pl.BlockSpec(a.shape, lambda i: (0,) * a.ndim)
    return pl.pallas_call(
        _outproj_kernel,
        grid=(n // tm,),
        in_specs=[row(D_MODEL), row(D_MODEL), row(D_MODEL), full(wa), full(wb), full(gn), full(wr), full(upper)],
        out_specs=[row(D_MODEL), pl.BlockSpec(_tiled(tm), lambda i: (i, 0)),
                   pl.BlockSpec((8, tm), lambda i: (0, i)), row(LANES),
                   pl.BlockSpec((N_EXPERTS, LANES), lambda i: (0, 0))],
        out_shape=[jax.ShapeDtypeStruct((n, D_MODEL), F32), jax.ShapeDtypeStruct(_tiled(n), F32),
                   jax.ShapeDtypeStruct((8, n), jnp.int32), jax.ShapeDtypeStruct((n, LANES), F32),
                   jax.ShapeDtypeStruct((N_EXPERTS, LANES), F32)],
        scratch_shapes=[pltpu.VMEM((N_EXPERTS, LANES), F32)],
        compiler_params=_params(("arbitrary",)),
        name="outproj_router",
    )(x, y_ssd, y_att, wa, wb, gn, wr, upper)


def _rows(ref, first, count):
    return ref.at[pl.ds(pl.multiple_of(first * ROW_SUB, ROW_SUB), count * ROW_SUB)]


def _row_copy(src_ref, src_row, dst_ref, dst_row, sem):
    return pltpu.make_async_copy(_rows(src_ref, src_row, 1), _rows(dst_ref, dst_row, 1), sem)


def _for_each_choice(dest_ref, n_rows, fn):
    def group(g, carry):
        base = pl.multiple_of(g * ROW_GROUP, ROW_GROUP)
        for jj in range(ROW_GROUP):
            for kk in range(2):
                fn(base + jj, kk, dest_ref[kk, base + jj])
        return carry

    lax.fori_loop(0, n_rows // ROW_GROUP, group, 0)


def _dispatch_kernel(pstart_ref, pend_ref, dest_ref, u_ref, slots_ref, zero_ref, sem):
    tm = u_ref.shape[0] // ROW_SUB

    @pl.when(pl.program_id(0) == 0)
    def _():
        zero_ref[...] = jnp.zeros_like(zero_ref)

        def fill(e):
            return pltpu.make_async_copy(zero_ref, _rows(slots_ref, pend_ref[e] - SLOT_BLOCK, SLOT_BLOCK), sem)

        for e in range(N_EXPERTS):
            @pl.when(pend_ref[e] > pstart_ref[e])
            def _():
                fill(e).start()
        for e in range(N_EXPERTS):
            @pl.when(pend_ref[e] > pstart_ref[e])
            def _():
                fill(e).wait()

        def tail(j):
            return pltpu.make_async_copy(zero_ref, _rows(slots_ref, j * SLOT_BLOCK, SLOT_BLOCK), sem)

        used = lax.div(pend_ref[N_EXPERTS - 1], SLOT_BLOCK)
        n_blocks = slots_ref.shape[0] // (SLOT_BLOCK * ROW_SUB)

        def tail_start(j, carry):
            tail(j).start()
            return carry

        def tail_wait(j, carry):
            tail(j).wait()
            return carry

        lax.fori_loop(used, n_blocks, tail_start, 0)
        lax.fori_loop(used, n_blocks, tail_wait, 0)

    _for_each_choice(dest_ref, tm, lambda r, kk, slot: _row_copy(u_ref, r, slots_ref, slot, sem).start(priority=kk))
    for _ in range(2):
        pltpu.make_async_copy(u_ref, _rows(slots_ref, 0, tm), sem).wait()


def _dispatch(u, dest, pad_start, pad_end, n_slots):
    n = u.shape[0] // ROW_SUB
    tm = ROUTE_TILE
    grid_spec = pltpu.PrefetchScalarGridSpec(
        num_scalar_prefetch=2,
        grid=(n // tm,),
        in_specs=[pl.BlockSpec((2, tm), lambda i, ps, pe: (0, i), memory_space=pltpu.SMEM),
                  pl.BlockSpec(_tiled(tm), lambda i, ps, pe: (i, 0))],
        out_specs=pl.BlockSpec(memory_space=pl.ANY),
        scratch_shapes=[pltpu.VMEM(_tiled(SLOT_BLOCK), F32), pltpu.SemaphoreType.DMA(())],
    )
    return pl.pallas_call(
        _dispatch_kernel,
        grid_spec=grid_spec,
        out_shape=jax.ShapeDtypeStruct(_tiled(n_slots), F32),
        compiler_params=_params(("arbitrary",)),
        name="dispatch",
    )(pad_start, pad_end, dest, u)


def _expert_kernel(blk_e_ref, used_ref, x_ref, wg_ref, wu_ref, wd_ref, y_ref):
    del blk_e_ref

    @pl.when(pl.program_id(0) < used_ref[0])
    def _():
        x = _load_row_tiles(x_ref).astype(BF16)
        a = _dot(x, wg_ref[0])
        b = _dot(x, wu_ref[0])
        hid = (a * (1.0 / (1.0 + jnp.exp(-a))) * b).astype(BF16)
        _store_row_tiles(y_ref, _dot(hid, wd_ref[0]))

    @pl.when(pl.program_id(0) >= used_ref[0])
    def _():
        y_ref[...] = jnp.zeros_like(y_ref)


def _experts(slots, blk_e, used, wg, wu, wd):
    n_slots = slots.shape[0] // ROW_SUB
    nb = n_slots // SLOT_BLOCK
    xmap = lambda i, be, us: (jnp.minimum(i, us[0] - 1), 0)
    wmap = lambda i, be, us: (be[i], 0, 0)
    grid_spec = pltpu.PrefetchScalarGridSpec(
        num_scalar_prefetch=2,
        grid=(nb,),
        in_specs=[pl.BlockSpec(_tiled(SLOT_BLOCK), xmap),
                  pl.BlockSpec((1, D_MODEL, D_EXPERT), wmap),
                  pl.BlockSpec((1, D_MODEL, D_EXPERT), wmap),
                  pl.BlockSpec((1, D_EXPERT, D_MODEL), wmap)],
        out_specs=pl.BlockSpec(_tiled(SLOT_BLOCK), lambda i, be, us: (i, 0)),
    )
    return pl.pallas_call(
        _expert_kernel,
        grid_spec=grid_spec,
        out_shape=jax.ShapeDtypeStruct(_tiled(n_slots), F32),
        compiler_params=_params(("arbitrary",)),
        name="experts",
    )(blk_e, used, slots, wg, wu, wd)


def _combine_kernel(dest_ref, dest_next_ref, h_ref, gate_ref, p_ref, wp_ref, wg_ref, gp_ref,
                    gf_ref, y_ref, out_ref, buf, sems):
    tm = h_ref.shape[0]
    i = pl.program_id(0)

    def gather(dref, which):
        _for_each_choice(dref, tm, lambda r, kk, slot: _row_copy(y_ref, slot, buf.at[which, kk], r,
                                                                 sems.at[which]).start(priority=kk))

    @pl.when(i == 0)
    def _():
        gather(dest_ref, 0)

    @pl.when(i + 1 < pl.num_programs(0))
    def _():
        gather(dest_next_ref, (i + 1) % 2)

    cur = i % 2
    for kk in range(2):
        pltpu.make_async_copy(_rows(y_ref, 0, tm), buf.at[cur, kk], sems.at[cur]).wait()

    gates = gate_ref[...]
    moe = gates[:, 0:1] * _load_row_tiles(buf.at[cur, 0]) + gates[:, 1:2] * _load_row_tiles(buf.at[cur, 1])
    h = h_ref[...] + moe
    u = _rms(h, gp_ref[...]).astype(BF16)
    gate = 1.0 / (1.0 + jnp.exp(-_dot(u, wg_ref[...])))
    h = h + _dot(p_ref[...].astype(BF16), wp_ref[...]) * gate
    out_ref[...] = _rms(h, gf_ref[...])


def _combine(dest, h, gate_col, p, wp, wg, gp, gf, y_sorted):
    n = h.shape[0]
    tm = ROUTE_TILE
    nt = n // tm
    row = lambda w: pl.BlockSpec((tm, w), lambda i: (i, 0))
    full = lambda a: pl.BlockSpec(a.shape, lambda i: (0,) * a.ndim)
    return pl.pallas_call(
        _combine_kernel,
        grid=(nt,),
        in_specs=[pl.BlockSpec((2, tm), lambda i: (0, i), memory_space=pltpu.SMEM),
                  pl.BlockSpec((2, tm), lambda i: (0, jnp.minimum(i + 1, nt - 1)), memory_space=pltpu.SMEM),
                  row(D_MODEL), row(LANES), row(PLE_DIM), full(wp), full(wg), full(gp), full(gf),
                  pl.BlockSpec(memory_space=pl.ANY)],
        out_specs=row(D_MODEL),
        out_shape=jax.ShapeDtypeStruct((n, D_MODEL), F32),
        scratch_shapes=[pltpu.VMEM((2, 2) + _tiled(tm), F32), pltpu.SemaphoreType.DMA((2,))],
        compiler_params=_params(("arbitrary",)),
        name="combine_ple",
    )(dest, dest, h, gate_col, p, wp, wg, gp, gf, y_sorted)


def _pad_lanes(v, width=LANES):
    return jnp.zeros((1, width), F32).at[0, :v.shape[0]].set(v.astype(F32))


def _encoder(x3, p3, prm):
    batch, seq, _ = x3.shape
    n = batch * seq
    assert seq % SSD_TILE == 0 and (seq // GRID_W) % NA_ROWS == 0 and n % TOKEN_TILE == 0
    x = x3.reshape(n, D_MODEL)
    p = p3.reshape(n, PLE_DIM)

    z, xbc, q, k, v, dt = _inproj(x, prm["norm_mix"], prm["w_z"], prm["w_xbc"], prm["w_q"], prm["w_k"],
                                  prm["w_v"], prm["w_dt"])

    y_bwd, act = _ssd_pass(xbc, dt, prm["dt_bias"][1], prm["a_log"][1], batch=batch, seq=seq, backward=True,
                           extra=(prm["conv_w"], prm["conv_b"]))
    y_ssd = _ssd_pass(act, dt, prm["dt_bias"][0], prm["a_log"][0], batch=batch, seq=seq, backward=False,
                      extra=(z, y_bwd, prm["d_exp"], prm["ssd_norm"]))
    y_att = _natten(q, k, v, prm["bias_tab"], prm["attn_norm"], batch=batch, seq=seq)

    h1, u2, route, gate_col, counts = _outproj(x, y_ssd, y_att, prm["w_out_a"], prm["w_out_b"], prm["norm_ffn"],
                                               prm["w_router"], prm["upper"])

    cnt = counts[:, 0].astype(jnp.int32)
    padded = (cnt + SLOT_BLOCK - 1) // SLOT_BLOCK * SLOT_BLOCK
    pad_end = jnp.cumsum(padded)
    pad_start = pad_end - padded
    n_blocks = -(-(2 * n + N_EXPERTS * (SLOT_BLOCK - 1)) // SLOT_BLOCK)
    n_slots = n_blocks * SLOT_BLOCK
    first_slot = jnp.arange(n_blocks, dtype=jnp.int32) * SLOT_BLOCK
    blk_e = jnp.minimum(jnp.sum((pad_end[None, :] <= first_slot[:, None]).astype(jnp.int32), axis=1), N_EXPERTS - 1)
    used = pad_end[-1:] // SLOT_BLOCK
    hot = route[0:2][None] == jnp.arange(N_EXPERTS, dtype=jnp.int32)[:, None, None]
    dest = route[2:4] + jnp.sum(jnp.where(hot, pad_start[:, None, None], 0), axis=0)

    slots = _dispatch(u2, dest, pad_start, pad_end, n_slots)
    y_sorted = _experts(slots, blk_e, used, prm["w_gate"], prm["w_up"], prm["w_down"])
    out = _combine(dest, h1, gate_col, p, prm["w_ple_proj"], prm["w_ple_gate"], prm["norm_ple"],
                   prm["norm_final"], y_sorted)
    return out.reshape(batch, seq, D_MODEL)


def _prepare(w_in, conv_w, conv_b, dt_bias, a_log, d_skip, ssd_norm, rel_bias, attn_norm, w_out, norm_mix,
             norm_ffn, router_group, router_expert, w_gate, w_up, w_down, norm_ple, w_ple_proj, w_ple_gate,
             norm_final):
    w_in = w_in[0]
    offs = np.cumsum([0, SSD_WIDTH, CONV_DIM, 2 * N_HEADS, D_MODEL, D_MODEL, D_MODEL])
    piece = lambda i: w_in[:, offs[i]:offs[i + 1]]
    w_dt = jnp.zeros((D_MODEL, 2 * LANES), F32)
    w_dt = w_dt.at[:, :N_HEADS].set(piece(2)[:, :N_HEADS]).at[:, LANES:LANES + N_HEADS].set(piece(2)[:, N_HEADS:])
    row = lambda v: v.astype(F32).reshape(1, -1)
    w_router = jnp.zeros((ROUTER_ROWS, D_MODEL), F32)
    w_router = w_router.at[:N_GROUPS].set(router_group[0].T)
    w_router = w_router.at[8:8 + N_EXPERTS].set(router_expert[0].transpose(0, 2, 1).reshape(N_EXPERTS, D_MODEL))
    tm = TOKEN_TILE
    upper = jnp.asarray(np.triu(np.ones((tm, tm), np.float32), k=1), BF16)
    return {
        "norm_mix": row(norm_mix[0]),
        "w_z": piece(0).astype(BF16), "w_xbc": piece(1).astype(BF16), "w_dt": w_dt.astype(BF16),
        "w_q": piece(3).astype(BF16), "w_k": piece(4).astype(BF16), "w_v": piece(5).astype(BF16),
        "conv_w": jnp.zeros((8, CONV_DIM), F32).at[:D_CONV].set(conv_w[0]),
        "conv_b": row(conv_b[0]),
        "dt_bias": [_pad_lanes(dt_bias[0, d]) for d in range(2)],
        "a_log": [jnp.full((1, LANES), -jnp.inf, F32).at[0, :N_HEADS].set(a_log[0, d]) for d in range(2)],
        "d_exp": jnp.repeat(d_skip[0].astype(F32), HEAD_DIM).reshape(1, SSD_WIDTH),
        "ssd_norm": row(ssd_norm[0]),
        "bias_tab": _bias_table(rel_bias[0]),
        "attn_norm": row(attn_norm[0]),
        "w_out_a": w_out[0, :SSD_WIDTH].astype(BF16), "w_out_b": w_out[0, SSD_WIDTH:].astype(BF16),
        "norm_ffn": row(norm_ffn[0]),
        "w_router": w_router,
        "upper": upper,
        "w_gate": w_gate[0].astype(BF16), "w_up": w_up[0].astype(BF16), "w_down": w_down[0].astype(BF16),
        "norm_ple": row(norm_ple[0]),
        "w_ple_proj": w_ple_proj[0].astype(BF16), "w_ple_gate": w_ple_gate[0].astype(BF16),
        "norm_final": row(norm_final),
    }


def kernel(x_prompt, x_sample, p_prompt, p_sample, w_in, conv_w, conv_b, dt_bias, a_log, d_skip, ssd_norm,
           rel_bias, attn_norm, w_out, norm_mix, norm_ffn, router_group, router_expert, w_gate, w_up, w_down,
           norm_ple, w_ple_proj, w_ple_gate, norm_final):
    prm = _prepare(w_in, conv_w, conv_b, dt_bias, a_log, d_skip, ssd_norm, rel_bias, attn_norm, w_out, norm_mix,
                   norm_ffn, router_group, router_expert, w_gate, w_up, w_down, norm_ple, w_ple_proj, w_ple_gate,
                   norm_final)
    y_prompt = _encoder(x_prompt, p_prompt[0], prm)
    y_sample = _encoder(x_sample, p_sample[0], prm)
    return (y_prompt, y_sample)
```

```python
import functools
import math

import numpy as np
import jax
import jax.numpy as jnp
from jax import lax
from jax.experimental import pallas as pl
from jax.experimental.pallas import tpu as pltpu

F32 = jnp.float32
BF16 = jnp.bfloat16

D_MODEL = 1024
SSD_WIDTH = 1024
HEAD_DIM = 64
N_HEADS = 16
N_BC_GROUPS = 2
D_STATE = 128
D_CONV = 5
CHUNK = 128
CONV_DIM = SSD_WIDTH + 2 * N_BC_GROUPS * D_STATE
HEADS_PER_GROUP = N_HEADS // N_BC_GROUPS
GROUP_WIDTH = SSD_WIDTH // N_BC_GROUPS
GRID_W = 64
WIN_H = 8
WIN_W = 16
N_GROUPS = 4
EXPERTS_PER_GROUP = 8
N_EXPERTS = N_GROUPS * EXPERTS_PER_GROUP
D_EXPERT = 512
PLE_DIM = 256
EPS = 1e-6
LOG2E = math.log2(math.e)

LANES = 128
HALO_ROWS = 16
TOKEN_TILE = 512
SSD_TILE = 512
CONV_COLS = 512
SSD_TRIP_CHUNKS = 2
NA_ROWS = 8
NA_TRIP_ROWS = 4
SLOT_BLOCK = 512
ROUTE_TILE = 256
ROW_GROUP = 8
ROW_SUB = 8
ROUTER_ROWS = 48
VMEM_LIMIT = 56 * 1024 * 1024


def _rms(x, g):
    ms = jnp.mean(x * x, axis=-1, keepdims=True)
    return x * lax.rsqrt(ms + EPS) * g


def _split3(x):
    hi = x.astype(BF16)
    r1 = x - hi.astype(F32)
    mid = r1.astype(BF16)
    lo = (r1 - mid.astype(F32)).astype(BF16)
    return hi, mid, lo


def _dot(a, b):
    return jnp.dot(a, b, preferred_element_type=F32)


def _dot_nt(a, b):
    return lax.dot_general(a, b, (((1,), (1,)), ((), ())), preferred_element_type=F32)


def _tiled(m):
    return (m * ROW_SUB, LANES)


def _dot_tn(a, b):
    return lax.dot_general(a, b, (((0,), (0,)), ((), ())), preferred_element_type=F32)


def _store_row_tiles(ref, x):
    m = x.shape[0]
    for s in range(ROW_SUB):
        ref[pl.ds(s, m, stride=ROW_SUB), :] = x[:, s * LANES:(s + 1) * LANES]


def _load_row_tiles(ref):
    m = ref.shape[0] // ROW_SUB
    return jnp.concatenate([ref[pl.ds(s, m, stride=ROW_SUB), :] for s in range(ROW_SUB)], axis=1)


def _params(sem):
    return pltpu.CompilerParams(dimension_semantics=sem, vmem_limit_bytes=VMEM_LIMIT)


def _inproj_kernel(x_ref, g_ref, wz_ref, wxbc_ref, wq_ref, wk_ref, wv_ref, wdt_ref,
                   z_ref, xbc_ref, q_ref, k_ref, v_ref, dt_ref):
    u = _rms(x_ref[...], g_ref[...]).astype(BF16)
    z_ref[...] = _dot(u, wz_ref[...]).astype(BF16)
    xbc_ref[...] = _dot(u, wxbc_ref[...]).astype(BF16)
    q_ref[...] = (_dot(u, wq_ref[...]) * (HEAD_DIM ** -0.5 * LOG2E)).astype(BF16)
    k_ref[...] = _dot(u, wk_ref[...]).astype(BF16)
    v_ref[...] = _dot(u, wv_ref[...]).astype(BF16)
    dt_ref[...] = _dot(u, wdt_ref[...])


def _inproj(x, g, wz, wxbc, wq, wk, wv, wdt):
    n = x.shape[0]
    tm = TOKEN_TILE
    row = lambda w: pl.BlockSpec((tm, w), lambda i: (i, 0))
    full = lambda a: pl.BlockSpec(a.shape, lambda i: (0,) * a.ndim)
    outs = [(D_MODEL, BF16), (CONV_DIM, BF16), (D_MODEL, BF16), (D_MODEL, BF16), (D_MODEL, BF16),
            (2 * LANES, F32)]
    return pl.pallas_call(
        _inproj_kernel,
        grid=(n // tm,),
        in_specs=[row(D_MODEL), full(g), full(wz), full(wxbc), full(wq), full(wk), full(wv), full(wdt)],
        out_specs=[row(w) for w, _ in outs],
        out_shape=[jax.ShapeDtypeStruct((n, w), d) for w, d in outs],
        compiler_params=_params(("parallel",)),
        name="inproj",
    )(x, g, wz, wxbc, wq, wk, wv, wdt)


def _ssd_kernel(*refs, backward, n_blocks):
    if backward:
        (xbc_ref, prev_ref, next_ref, dt_ref, cw_ref, cb_ref, dtb_ref, alog_ref,
         out_ref, act_ref, xp_ref, st_ref) = refs
    else:
        (act_ref, dt_ref, dtb_ref, alog_ref, z_ref, ybwd_ref, dexp_ref, gn_ref, out_ref, st_ref) = refs
    tb = act_ref.shape[0]
    t = CHUNK
    n_chunks = tb // t
    j = pl.program_id(1)

    @pl.when(j == 0)
    def _():
        st_ref[...] = jnp.zeros_like(st_ref)

    if backward:
        blk = n_blocks - 1 - j
        hr = HALO_ROWS
        xp_ref[0:hr, :] = jnp.where(blk == 0, jnp.zeros_like(prev_ref), prev_ref[...])
        xp_ref[hr:hr + tb, :] = xbc_ref[...]
        xp_ref[hr + tb:, :] = jnp.where(blk == n_blocks - 1, jnp.zeros_like(next_ref), next_ref[...])
        taps = [kk for kk in range(D_CONV) if kk != D_CONV // 2]
        win = t + 2 * hr
        srow = lax.broadcasted_iota(jnp.int32, (len(taps) * t, win), 0)
        scol = lax.broadcasted_iota(jnp.int32, (len(taps) * t, win), 1)
        tap_id = srow // t
        tap_off = tap_id + jnp.where(tap_id >= D_CONV // 2, 1, 0) + (hr - D_CONV // 2)
        shift_sel = jnp.where(scol == (srow % t) + tap_off, 1.0, 0.0).astype(BF16)
        for c in range(n_chunks):
            for c0 in range(0, CONV_DIM, CONV_COLS):
                cols = slice(c0, c0 + CONV_COLS)
                window = xp_ref[c * t:c * t + win, cols]
                shifted = _dot(shift_sel, window)
                acc = cb_ref[:, cols] + cw_ref[D_CONV // 2:D_CONV // 2 + 1, cols] * window[hr:hr + t].astype(F32)
                for ti, kk in enumerate(taps):
                    acc = acc + cw_ref[kk:kk + 1, cols] * shifted[ti * t:(ti + 1) * t]
                act_ref[c * t:(c + 1) * t, cols] = (acc * (1.0 / (1.0 + jnp.exp(-acc)))).astype(act_ref.dtype)

    a_row = -jnp.exp(alog_ref[...]) * LOG2E
    dtb = dtb_ref[...]
    ri = lax.broadcasted_iota(jnp.int32, (t, t), 0)
    ci = lax.broadcasted_iota(jnp.int32, (t, t), 1)
    if backward:
        keep = ci >= ri
    else:
        keep = ri >= ci
    tri = jnp.where(keep, 1.0, 0.0).astype(BF16)
    er = lax.broadcasted_iota(jnp.int32, (2 * LANES, SSD_WIDTH), 0)
    ec = lax.broadcasted_iota(jnp.int32, (2 * LANES, SSD_WIDTH), 1)
    expand = jnp.where((er % LANES) == (ec // HEAD_DIM), 1.0, 0.0).astype(BF16)
    lane = lax.broadcasted_iota(jnp.int32, (t, LANES), 1)
    low_half = lane < HEAD_DIM

    def chunk(c, st):
        r0 = pl.multiple_of(c * t, t)
        act = act_ref[pl.ds(r0, t), :]
        xs_bf = act[:, :SSD_WIDTH]
        xs = xs_bf.astype(F32)
        groups = []
        for g in range(N_BC_GROUPS):
            b_g = act[:, SSD_WIDTH + g * D_STATE:SSD_WIDTH + (g + 1) * D_STATE]
            c_g = act[:, SSD_WIDTH + (N_BC_GROUPS + g) * D_STATE:SSD_WIDTH + (N_BC_GROUPS + g + 1) * D_STATE]
            groups.append((b_g, c_g, slice(g * GROUP_WIDTH, (g + 1) * GROUP_WIDTH)))

        st_bf = st.astype(BF16)
        cbs = [_dot_nt(c_g, b_g) for b_g, c_g, _ in groups]
        y_offs = [_dot(c_g, st_bf[:, hs]) for _, c_g, hs in groups]

        dt_raw = dt_ref[pl.ds(r0, t), :] + dtb
        dt = jnp.maximum(dt_raw, 0.0) + jnp.log(1.0 + jnp.exp(-jnp.abs(dt_raw)))
        hi, mid, lo = _split3(dt * a_row)
        cs = _dot(tri, hi) + _dot(tri, mid) + _dot(tri, lo)
        cs_row = cs.T
        dt_row = dt.T
        edge = cs[0:1, :] if backward else cs[t - 1:t, :]
        w_in = dt * jnp.exp2(edge - cs)
        ecs = jnp.exp2(cs)
        both = jnp.concatenate([w_in, ecs], axis=0)
        b_hi = both.astype(BF16)
        b_lo = (both - b_hi.astype(F32)).astype(BF16)
        both_x = _dot(jnp.concatenate([b_hi, b_lo], axis=1), expand)
        w_x = both_x[:t]
        ecs_x = both_x[t:]
        cd_x = ecs_x[0:1, :] if backward else ecs_x[t - 1:t, :]

        y_parts = []
        for g, (_, _, hs) in enumerate(groups):
            pair_parts = []
            for pp in range(HEADS_PER_GROUP // 2):
                h0 = g * HEADS_PER_GROUP + 2 * pp
                xs_pair = xs_bf[:, h0 * HEAD_DIM:(h0 + 2) * HEAD_DIM]
                ys = []
                for h in (h0, h0 + 1):
                    diff = cs[:, h:h + 1] - cs_row[h:h + 1, :]
                    seg = jnp.exp2(jnp.where(keep, diff, -jnp.inf))
                    m = (cbs[g] * seg * dt_row[h:h + 1, :]).astype(BF16)
                    ys.append(_dot(m, xs_pair))
                pair_parts.append(jnp.where(low_half, ys[0], ys[1]))
            y_parts.append(jnp.concatenate(pair_parts, axis=1) + y_offs[g] * ecs_x[:, hs])
        y = jnp.concatenate(y_parts, axis=1)

        xw = (xs * w_x).astype(BF16)
        news = [_dot_tn(b_g, xw[:, hs]) for b_g, _, hs in groups]
        st_new = st * cd_x + jnp.concatenate(news, axis=1)

        if backward:
            out_ref[pl.ds(r0, t), :] = y.astype(out_ref.dtype)
        else:
            y = y + ybwd_ref[pl.ds(r0, t), :].astype(F32) + dexp_ref[...] * xs
            zz = z_ref[pl.ds(r0, t), :].astype(F32)
            y = y * (zz * (1.0 / (1.0 + jnp.exp(-zz))))
            out_ref[pl.ds(r0, t), :] = _rms(y, gn_ref[...]).astype(out_ref.dtype)
        return st_new

    def trip(i, carry):
        st = st_ref[...]
        for u in range(SSD_TRIP_CHUNKS):
            ci = i * SSD_TRIP_CHUNKS + u
            st = chunk((n_chunks - 1 - ci) if backward else ci, st)
        st_ref[...] = st
        return carry

    lax.fori_loop(0, n_chunks // SSD_TRIP_CHUNKS, trip, 0)


def _ssd_pass(xin, dt, dtb, alog, *, batch, seq, backward, extra=()):
    n = xin.shape[0]
    tb = SSD_TILE
    nb = seq // tb
    hb = tb // HALO_ROWS
    n_halo = n // HALO_ROWS
    d = 1 if backward else 0

    def blk_of(b, j):
        return b * nb + ((nb - 1 - j) if backward else j)

    main = lambda w: pl.BlockSpec((tb, w), lambda b, j: (blk_of(b, j), 0))
    full = lambda a: pl.BlockSpec(a.shape, lambda b, j: (0,) * a.ndim)
    dt_spec = pl.BlockSpec((tb, LANES), lambda b, j: (blk_of(b, j), d))
    state = pltpu.VMEM((D_STATE, SSD_WIDTH), F32)
    y_shape = jax.ShapeDtypeStruct((n, SSD_WIDTH), BF16)
    if backward:
        cw, cb = extra
        in_specs = [
            main(CONV_DIM),
            pl.BlockSpec((HALO_ROWS, CONV_DIM), lambda b, j: (jnp.maximum(blk_of(b, j) * hb - 1, 0), 0)),
            pl.BlockSpec((HALO_ROWS, CONV_DIM), lambda b, j: (jnp.minimum((blk_of(b, j) + 1) * hb, n_halo - 1), 0)),
            dt_spec, full(cw), full(cb), full(dtb), full(alog),
        ]
        args = [xin, xin, xin, dt, cw, cb, dtb, alog]
        out_specs = [main(SSD_WIDTH), main(CONV_DIM)]
        out_shape = [y_shape, jax.ShapeDtypeStruct((n, CONV_DIM), BF16)]
        scratch = [pltpu.VMEM((tb + 2 * HALO_ROWS, CONV_DIM), BF16), state]
    else:
        z, ybwd, dexp, gn = extra
        in_specs = [main(CONV_DIM), dt_spec, full(dtb), full(alog), main(SSD_WIDTH), main(SSD_WIDTH), full(dexp),
                    full(gn)]
        args = [xin, dt, dtb, alog, z, ybwd, dexp, gn]
        out_specs = main(SSD_WIDTH)
        out_shape = y_shape
        scratch = [state]
    return pl.pallas_call(
        functools.partial(_ssd_kernel, backward=backward, n_blocks=nb),
        grid=(batch, nb),
        in_specs=in_specs,
        out_specs=out_specs,
        out_shape=out_shape,
        scratch_shapes=scratch,
        compiler_params=_params(("parallel", "arbitrary")),
        name="ssd_bwd" if backward else "ssd_fwd",
    )(*args)


def _natten_kernel(q_ref, kp_ref, kc_ref, kn_ref, vp_ref, vc_ref, vn_ref, bias_ref, gn_ref,
                   out_ref, kbuf, vbuf, obuf, *, rows):
    half = NA_ROWS // 2 * GRID_W
    full = NA_ROWS * GRID_W
    jb = pl.program_id(1)
    kbuf[0:half, :] = kp_ref[...]
    kbuf[half:half + full, :] = kc_ref[...]
    kbuf[half + full:, :] = kn_ref[...]
    vbuf[0:half, :] = vp_ref[...]
    vbuf[half:half + full, :] = vc_ref[...]
    vbuf[half + full:, :] = vn_ref[...]
    lane = lax.broadcasted_iota(jnp.int32, (GRID_W, LANES), 1)
    low_half = lane < HEAD_DIM
    n_keys = WIN_H * GRID_W

    n_pairs = N_HEADS // 2
    lanes_of = lambda pp: slice(pp * LANES, (pp + 1) * LANES)

    def trip_body(it, carry):
        geo = []
        for rr in range(NA_TRIP_ROWS):
            i = it * NA_TRIP_ROWS + rr
            r = jb * NA_ROWS + i
            s = jnp.clip(r - WIN_H // 2, 0, rows - WIN_H)
            local = s - (jb * NA_ROWS - NA_ROWS // 2)
            geo.append((pl.multiple_of(local * GRID_W, GRID_W), pl.multiple_of(i * GRID_W, GRID_W),
                        s - r + (WIN_H - 1)))
        scores = []
        for k0, q0, d0 in geo:
            for pp in range(n_pairs):
                q_pair = q_ref[pl.ds(q0, GRID_W), lanes_of(pp)]
                zero = jnp.zeros_like(q_pair)
                q2 = jnp.concatenate([jnp.where(low_half, q_pair, zero), jnp.where(low_half, zero, q_pair)], axis=0)
                sc = _dot_nt(q2, kbuf[pl.ds(k0, n_keys), lanes_of(pp)])
                bias = jnp.concatenate(
                    [jnp.concatenate([bias_ref[d0 + 2 * ii, 2 * pp + hh] for ii in range(WIN_H // 2)], axis=1)
                     for hh in range(2)], axis=0)
                scores.append(sc + bias)
        probs = []
        for sc in scores:
            m = jnp.max(sc, axis=-1, keepdims=True)
            e = jnp.exp2(sc - m)
            probs.append((e.astype(BF16), 1.0 / jnp.sum(e, axis=-1, keepdims=True)))
        for rr, (k0, q0, d0) in enumerate(geo):
            for pp in range(n_pairs):
                e, inv = probs[rr * n_pairs + pp]
                o2 = _dot(e, vbuf[pl.ds(k0, n_keys), lanes_of(pp)]) * inv
                obuf[rr * GRID_W:(rr + 1) * GRID_W, lanes_of(pp)] = jnp.where(low_half, o2[:GRID_W], o2[GRID_W:])
        for rr, (k0, q0, d0) in enumerate(geo):
            o = obuf[rr * GRID_W:(rr + 1) * GRID_W, :]
            out_ref[pl.ds(q0, GRID_W), :] = _rms(o, gn_ref[...]).astype(out_ref.dtype)
        return carry

    lax.fori_loop(0, NA_ROWS // NA_TRIP_ROWS, trip_body, 0)


def _natten(q, k, v, bias_tab, gn, *, batch, seq):
    n = q.shape[0]
    rows = seq // GRID_W
    nrb = rows // NA_ROWS
    full = NA_ROWS * GRID_W
    half = full // 2
    n_half = n // half
    cur = pl.BlockSpec((full, D_MODEL), lambda b, j: (b * nrb + j, 0))
    prev = pl.BlockSpec((half, D_MODEL), lambda b, j: (jnp.maximum((b * nrb + j) * 2 - 1, 0), 0))
    nxt = pl.BlockSpec((half, D_MODEL), lambda b, j: (jnp.minimum((b * nrb + j) * 2 + 2, n_half - 1), 0))
    const = lambda a: pl.BlockSpec(a.shape, lambda b, j: (0,) * a.ndim)
    return pl.pallas_call(
        functools.partial(_natten_kernel, rows=rows),
        grid=(batch, nrb),
        in_specs=[cur, prev, cur, nxt, prev, cur, nxt, const(bias_tab), const(gn)],
        out_specs=cur,
        out_shape=jax.ShapeDtypeStruct((n, D_MODEL), BF16),
        scratch_shapes=[pltpu.VMEM((2 * full, D_MODEL), BF16), pltpu.VMEM((2 * full, D_MODEL), BF16),
                        pltpu.VMEM((NA_TRIP_ROWS * GRID_W, D_MODEL), F32)],
        compiler_params=_params(("parallel", "parallel")),
        name="natten",
    )(q, k, k, k, v, v, v, bias_tab, gn)


def _bias_table(rel_bias):
    qc = np.arange(GRID_W)[:, None]
    kc = np.arange(GRID_W)[None, :]
    s_c = np.clip(qc - WIN_W // 2, 0, GRID_W - WIN_W)
    inside = (kc >= s_c) & (kc < s_c + WIN_W)
    dcol = np.clip(kc - qc, -(WIN_W - 1), WIN_W - 1) + WIN_W - 1
    tab = rel_bias.astype(F32)[:, :, dcol]
    tab = jnp.where(inside[None, None], tab * LOG2E, -jnp.inf)
    pairs = jnp.concatenate([tab[:, :-1], tab[:, 1:]], axis=-1)
    return pairs.transpose(1, 0, 2, 3)


def _outproj_kernel(x_ref, ys_ref, ya_ref, wa_ref, wb_ref, gn_ref, wr_ref, upper_ref,
                    h_ref, u_ref, idx_ref, gate_ref, cnt_ref, carry_ref):
    tm = x_ref.shape[0]

    @pl.when(pl.program_id(0) == 0)
    def _():
        carry_ref[...] = jnp.zeros_like(carry_ref)

    h = x_ref[...] + _dot(ys_ref[...], wa_ref[...]) + _dot(ya_ref[...], wb_ref[...])
    h_ref[...] = h
    u = _rms(h, gn_ref[...])
    _store_row_tiles(u_ref, u)

    u_hi = u.astype(BF16)
    u_lo = (u - u_hi.astype(F32)).astype(BF16)
    wr = wr_ref[...]
    w_hi = wr.astype(BF16)
    w_lo = (wr - w_hi.astype(F32)).astype(BF16)
    logit = _dot_nt(w_hi, u_hi) + _dot_nt(w_hi, u_lo) + _dot_nt(w_lo, u_hi)

    lg = logit[0:N_GROUPS]
    gi = lax.broadcasted_iota(jnp.int32, (N_GROUPS, tm), 0)
    g_max = jnp.max(lg, axis=0, keepdims=True)
    g_idx = jnp.min(jnp.where(lg == g_max, gi, N_GROUPS), axis=0, keepdims=True)
    g_val = 1.0 / jnp.sum(jnp.exp(lg - g_max), axis=0, keepdims=True)

    le = logit[8:8 + EXPERTS_PER_GROUP]
    for g in range(1, N_GROUPS):
        le = jnp.where(g_idx == g, logit[8 + g * EXPERTS_PER_GROUP:8 + (g + 1) * EXPERTS_PER_GROUP], le)
    ei = lax.broadcasted_iota(jnp.int32, (EXPERTS_PER_GROUP, tm), 0)
    m1 = jnp.max(le, axis=0, keepdims=True)
    i1 = jnp.min(jnp.where(le == m1, ei, EXPERTS_PER_GROUP), axis=0, keepdims=True)
    le2 = jnp.where(ei == i1, -jnp.inf, le)
    m2 = jnp.max(le2, axis=0, keepdims=True)
    i2 = jnp.min(jnp.where(le2 == m2, ei, EXPERTS_PER_GROUP), axis=0, keepdims=True)
    e21 = jnp.exp(m2 - m1)
    den = 1.0 / (1.0 + e21)
    gate1 = g_val * den
    gate2 = g_val * e21 * den
    e1 = g_idx * EXPERTS_PER_GROUP + i1
    e2 = g_idx * EXPERTS_PER_GROUP + i2

    xi = lax.broadcasted_iota(jnp.int32, (N_EXPERTS, tm), 0)
    hot1 = xi == e1
    hot2 = xi == e2
    hot = jnp.where(jnp.logical_or(hot1, hot2), 1.0, 0.0)
    before = _dot(hot.astype(BF16), upper_ref[...])
    carry = carry_ref[...]
    before = before + jnp.tile(carry, (1, tm // LANES))
    rank1 = jnp.sum(jnp.where(hot1, before, 0.0), axis=0, keepdims=True)
    rank2 = jnp.sum(jnp.where(hot2, before, 0.0), axis=0, keepdims=True)
    carry = carry + jnp.sum(hot, axis=1, keepdims=True)
    carry_ref[...] = carry
    cnt_ref[...] = carry

    zi = jnp.zeros((4, tm), jnp.int32)
    idx_ref[...] = jnp.concatenate([e1, e2, rank1.astype(jnp.int32), rank2.astype(jnp.int32), zi], axis=0)
    gates = jnp.concatenate([gate1, gate2, jnp.zeros((LANES - 2, tm), F32)], axis=0)
    gate_ref[...] = gates.T


def _outproj(x, y_ssd, y_att, wa, wb, gn, wr, upper):
    n = x.shape[0]
    tm = TOKEN_TILE
    row = lambda w: pl.BlockSpec((tm, w), lambda i: (i, 0))
    full = lambda a: pl.BlockSpec(a.shape, lambda i: (0,) * a.ndim)
    return pl.pallas_call(
        _outproj_kernel,
        grid=(n // tm,),
        in_specs=[row(D_MODEL), row(D_MODEL), row(D_MODEL), full(wa), full(wb), full(gn), full(wr), full(upper)],
        out_specs=[row(D_MODEL), pl.BlockSpec(_tiled(tm), lambda i: (i, 0)),
                   pl.BlockSpec((8, tm), lambda i: (0, i)), row(LANES),
                   pl.BlockSpec((N_EXPERTS, LANES), lambda i: (0, 0))],
        out_shape=[jax.ShapeDtypeStruct((n, D_MODEL), F32), jax.ShapeDtypeStruct(_tiled(n), F32),
                   jax.ShapeDtypeStruct((8, n), jnp.int32), jax.ShapeDtypeStruct((n, LANES), F32),
                   jax.ShapeDtypeStruct((N_EXPERTS, LANES), F32)],
        scratch_shapes=[pltpu.VMEM((N_EXPERTS, LANES), F32)],
        compiler_params=_params(("arbitrary",)),
        name="outproj_router",
    )(x, y_ssd, y_att, wa, wb, gn, wr, upper)


def _rows(ref, first, count):
    return ref.at[pl.ds(pl.multiple_of(first * ROW_SUB, ROW_SUB), count * ROW_SUB)]


def _row_copy(src_ref, src_row, dst_ref, dst_row, sem):
    return pltpu.make_async_copy(_rows(src_ref, src_row, 1), _rows(dst_ref, dst_row, 1), sem)


def _for_each_choice(dest_ref, n_rows, fn):
    def group(g, carry):
        base = pl.multiple_of(g * ROW_GROUP, ROW_GROUP)
        for jj in range(ROW_GROUP):
            for kk in range(2):
                fn(base + jj, kk, dest_ref[0, 0, kk * n_rows + base + jj])
        return carry

    lax.fori_loop(0, n_rows // ROW_GROUP, group, 0)


def _dispatch_kernel(pstart_ref, pend_ref, dest_ref, u_ref, slots_ref, zero_ref, sem):
    tm = u_ref.shape[0] // ROW_SUB

    @pl.when(pl.program_id(0) == 0)
    def _():
        zero_ref[...] = jnp.zeros_like(zero_ref)

        def fill(e):
            return pltpu.make_async_copy(zero_ref, _rows(slots_ref, pend_ref[e] - SLOT_BLOCK, SLOT_BLOCK), sem)

        for e in range(N_EXPERTS):
            @pl.when(pend_ref[e] > pstart_ref[e])
            def _():
                fill(e).start()
        for e in range(N_EXPERTS):
            @pl.when(pend_ref[e] > pstart_ref[e])
            def _():
                fill(e).wait()

        def tail(j):
            return pltpu.make_async_copy(zero_ref, _rows(slots_ref, j * SLOT_BLOCK, SLOT_BLOCK), sem)

        used = lax.div(pend_ref[N_EXPERTS - 1], SLOT_BLOCK)
        n_blocks = slots_ref.shape[0] // (SLOT_BLOCK * ROW_SUB)

        def tail_start(j, carry):
            tail(j).start()
            return carry

        def tail_wait(j, carry):
            tail(j).wait()
            return carry

        lax.fori_loop(used, n_blocks, tail_start, 0)
        lax.fori_loop(used, n_blocks, tail_wait, 0)

    _for_each_choice(dest_ref, tm, lambda r, kk, slot: _row_copy(u_ref, r, slots_ref, slot, sem).start(priority=kk))
    for _ in range(2):
        pltpu.make_async_copy(u_ref, _rows(slots_ref, 0, tm), sem).wait()


def _dispatch(u, dest, pad_start, pad_end, n_slots):
    n = u.shape[0] // ROW_SUB
    tm = ROUTE_TILE
    grid_spec = pltpu.PrefetchScalarGridSpec(
        num_scalar_prefetch=2,
        grid=(n // tm,),
        in_specs=[pl.BlockSpec((1, 1, 2 * tm), lambda i, ps, pe: (i, 0, 0), memory_space=pltpu.SMEM),
                  pl.BlockSpec(_tiled(tm), lambda i, ps, pe: (i, 0))],
        out_specs=pl.BlockSpec(memory_space=pl.ANY),
        scratch_shapes=[pltpu.VMEM(_tiled(SLOT_BLOCK), F32), pltpu.SemaphoreType.DMA(())],
    )
    return pl.pallas_call(
        _dispatch_kernel,
        grid_spec=grid_spec,
        out_shape=jax.ShapeDtypeStruct(_tiled(n_slots), F32),
        compiler_params=_params(("arbitrary",)),
        name="dispatch",
    )(pad_start, pad_end, dest, u)


def _expert_kernel(blk_e_ref, used_ref, x_ref, wg_ref, wu_ref, wd_ref, y_ref):
    del blk_e_ref

    @pl.when(pl.program_id(0) < used_ref[0])
    def _():
        x = _load_row_tiles(x_ref).astype(BF16)
        a = _dot(x, wg_ref[0])
        b = _dot(x, wu_ref[0])
        hid = (a * (1.0 / (1.0 + jnp.exp(-a))) * b).astype(BF16)
        _store_row_tiles(y_ref, _dot(hid, wd_ref[0]))

    @pl.when(pl.program_id(0) >= used_ref[0])
    def _():
        y_ref[...] = jnp.zeros_like(y_ref)


def _experts(slots, blk_e, used, wg, wu, wd):
    n_slots = slots.shape[0] // ROW_SUB
    nb = n_slots // SLOT_BLOCK
    xmap = lambda i, be, us: (jnp.minimum(i, us[0] - 1), 0)
    wmap = lambda i, be, us: (be[i], 0, 0)
    grid_spec = pltpu.PrefetchScalarGridSpec(
        num_scalar_prefetch=2,
        grid=(nb,),
        in_specs=[pl.BlockSpec(_tiled(SLOT_BLOCK), xmap),
                  pl.BlockSpec((1, D_MODEL, D_EXPERT), wmap),
                  pl.BlockSpec((1, D_MODEL, D_EXPERT), wmap),
                  pl.BlockSpec((1, D_EXPERT, D_MODEL), wmap)],
        out_specs=pl.BlockSpec(_tiled(SLOT_BLOCK), lambda i, be, us: (i, 0)),
    )
    return pl.pallas_call(
        _expert_kernel,
        grid_spec=grid_spec,
        out_shape=jax.ShapeDtypeStruct(_tiled(n_slots), F32),
        compiler_params=_params(("arbitrary",)),
        name="experts",
    )(blk_e, used, slots, wg, wu, wd)


def _combine_kernel(dest_ref, dest_next_ref, h_ref, gate_ref, p_ref, wp_ref, wg_ref, gp_ref,
                    gf_ref, y_ref, out_ref, buf, sems):
    tm = h_ref.shape[0]
    i = pl.program_id(0)

    def gather(dref, which):
        _for_each_choice(dref, tm, lambda r, kk, slot: _row_copy(y_ref, slot, buf.at[which, kk], r,
                                                                 sems.at[which]).start(priority=kk))

    @pl.when(i == 0)
    def _():
        gather(dest_ref, 0)

    @pl.when(i + 1 < pl.num_programs(0))
    def _():
        gather(dest_next_ref, (i + 1) % 2)

    cur = i % 2
    for kk in range(2):
        pltpu.make_async_copy(_rows(y_ref, 0, tm), buf.at[cur, kk], sems.at[cur]).wait()

    gates = gate_ref[...]
    moe = gates[:, 0:1] * _load_row_tiles(buf.at[cur, 0]) + gates[:, 1:2] * _load_row_tiles(buf.at[cur, 1])
    h = h_ref[...] + moe
    u = _rms(h, gp_ref[...]).astype(BF16)
    gate = 1.0 / (1.0 + jnp.exp(-_dot(u, wg_ref[...])))
    h = h + _dot(p_ref[...].astype(BF16), wp_ref[...]) * gate
    out_ref[...] = _rms(h, gf_ref[...])


def _combine(dest, h, gate_col, p, wp, wg, gp, gf, y_sorted):
    n = h.shape[0]
    tm = ROUTE_TILE
    nt = n // tm
    row = lambda w: pl.BlockSpec((tm, w), lambda i: (i, 0))
    full = lambda a: pl.BlockSpec(a.shape, lambda i: (0,) * a.ndim)
    return pl.pallas_call(
        _combine_kernel,
        grid=(nt,),
        in_specs=[pl.BlockSpec((1, 1, 2 * tm), lambda i: (i, 0, 0), memory_space=pltpu.SMEM),
                  pl.BlockSpec((1, 1, 2 * tm), lambda i: (jnp.minimum(i + 1, nt - 1), 0, 0), memory_space=pltpu.SMEM),
                  row(D_MODEL), row(LANES), row(PLE_DIM), full(wp), full(wg), full(gp), full(gf),
                  pl.BlockSpec(memory_space=pl.ANY)],
        out_specs=row(D_MODEL),
        out_shape=jax.ShapeDtypeStruct((n, D_MODEL), F32),
        scratch_shapes=[pltpu.VMEM((2, 2) + _tiled(tm), F32), pltpu.SemaphoreType.DMA((2,))],
        compiler_params=_params(("arbitrary",)),
        name="combine_ple",
    )(dest, dest, h, gate_col, p, wp, wg, gp, gf, y_sorted)


def _pad_lanes(v, width=LANES):
    return jnp.zeros((1, width), F32).at[0, :v.shape[0]].set(v.astype(F32))


def _encoder(x3, p3, prm):
    batch, seq, _ = x3.shape
    n = batch * seq
    assert seq % SSD_TILE == 0 and (seq // GRID_W) % NA_ROWS == 0 and n % TOKEN_TILE == 0
    x = x3.reshape(n, D_MODEL)
    p = p3.reshape(n, PLE_DIM)

    z, xbc, q, k, v, dt = _inproj(x, prm["norm_mix"], prm["w_z"], prm["w_xbc"], prm["w_q"], prm["w_k"],
                                  prm["w_v"], prm["w_dt"])

    y_bwd, act = _ssd_pass(xbc, dt, prm["dt_bias"][1], prm["a_log"][1], batch=batch, seq=seq, backward=True,
                           extra=(prm["conv_w"], prm["conv_b"]))
    y_ssd = _ssd_pass(act, dt, prm["dt_bias"][0], prm["a_log"][0], batch=batch, seq=seq, backward=False,
                      extra=(z, y_bwd, prm["d_exp"], prm["ssd_norm"]))
    y_att = _natten(q, k, v, prm["bias_tab"], prm["attn_norm"], batch=batch, seq=seq)

    h1, u2, route, gate_col, counts = _outproj(x, y_ssd, y_att, prm["w_out_a"], prm["w_out_b"], prm["norm_ffn"],
                                               prm["w_router"], prm["upper"])

    cnt = counts[:, 0].astype(jnp.int32)
    padded = (cnt + SLOT_BLOCK - 1) // SLOT_BLOCK * SLOT_BLOCK
    pad_end = jnp.cumsum(padded)
    pad_start = pad_end - padded
    n_blocks = -(-(2 * n + N_EXPERTS * (SLOT_BLOCK - 1)) // SLOT_BLOCK)
    n_slots = n_blocks * SLOT_BLOCK
    first_slot = jnp.arange(n_blocks, dtype=jnp.int32) * SLOT_BLOCK
    blk_e = jnp.minimum(jnp.sum((pad_end[None, :] <= first_slot[:, None]).astype(jnp.int32), axis=1), N_EXPERTS - 1)
    used = pad_end[-1:] // SLOT_BLOCK
    hot = route[0:2][None] == jnp.arange(N_EXPERTS, dtype=jnp.int32)[:, None, None]
    dest = route[2:4] + jnp.sum(jnp.where(hot, pad_start[:, None, None], 0), axis=0)
    nt = n // ROUTE_TILE
    dest = dest.reshape(2, nt, ROUTE_TILE).transpose(1, 0, 2).reshape(nt, 1, 2 * ROUTE_TILE)

    slots = _dispatch(u2, dest, pad_start, pad_end, n_slots)
    y_sorted = _experts(slots, blk_e, used, prm["w_gate"], prm["w_up"], prm["w_down"])
    out = _combine(dest, h1, gate_col, p, prm["w_ple_proj"], prm["w_ple_gate"], prm["norm_ple"],
                   prm["norm_final"], y_sorted)
    return out.reshape(batch, seq, D_MODEL)


def _prepare(w_in, conv_w, conv_b, dt_bias, a_log, d_skip, ssd_norm, rel_bias, attn_norm, w_out, norm_mix,
             norm_ffn, router_group, router_expert, w_gate, w_up, w_down, norm_ple, w_ple_proj, w_ple_gate,
             norm_final):
    w_in = w_in[0]
    offs = np.cumsum([0, SSD_WIDTH, CONV_DIM, 2 * N_HEADS, D_MODEL, D_MODEL, D_MODEL])
    piece = lambda i: w_in[:, offs[i]:offs[i + 1]]
    w_dt = jnp.zeros((D_MODEL, 2 * LANES), F32)
    w_dt = w_dt.at[:, :N_HEADS].set(piece(2)[:, :N_HEADS]).at[:, LANES:LANES + N_HEADS].set(piece(2)[:, N_HEADS:])
    row = lambda v: v.astype(F32).reshape(1, -1)
    w_router = jnp.zeros((ROUTER_ROWS, D_MODEL), F32)
    w_router = w_router.at[:N_GROUPS].set(router_group[0].T)
    w_router = w_router.at[8:8 + N_EXPERTS].set(router_expert[0].transpose(0, 2, 1).reshape(N_EXPERTS, D_MODEL))
    tm = TOKEN_TILE
    upper = jnp.asarray(np.triu(np.ones((tm, tm), np.float32), k=1), BF16)
    return {
        "norm_mix": row(norm_mix[0]),
        "w_z": piece(0).astype(BF16), "w_xbc": piece(1).astype(BF16), "w_dt": w_dt.astype(BF16),
        "w_q": piece(3).astype(BF16), "w_k": piece(4).astype(BF16), "w_v": piece(5).astype(BF16),
        "conv_w": jnp.zeros((8, CONV_DIM), F32).at[:D_CONV].set(conv_w[0]),
        "conv_b": row(conv_b[0]),
        "dt_bias": [_pad_lanes(dt_bias[0, d]) for d in range(2)],
        "a_log": [jnp.full((1, LANES), -jnp.inf, F32).at[0, :N_HEADS].set(a_log[0, d]) for d in range(2)],
        "d_exp": jnp.repeat(d_skip[0].astype(F32), HEAD_DIM).reshape(1, SSD_WIDTH),
        "ssd_norm": row(ssd_norm[0]),
        "bias_tab": _bias_table(rel_bias[0]),
        "attn_norm": row(attn_norm[0]),
        "w_out_a": w_out[0, :SSD_WIDTH].astype(BF16), "w_out_b": w_out[0, SSD_WIDTH:].astype(BF16),
        "norm_ffn": row(norm_ffn[0]),
        "w_router": w_router,
        "upper": upper,
        "w_gate": w_gate[0].astype(BF16), "w_up": w_up[0].astype(BF16), "w_down": w_down[0].astype(BF16),
        "norm_ple": row(norm_ple[0]),
        "w_ple_proj": w_ple_proj[0].astype(BF16), "w_ple_gate": w_ple_gate[0].astype(BF16),
        "norm_final": row(norm_final),
    }


def kernel(x_prompt, x_sample, p_prompt, p_sample, w_in, conv_w, conv_b, dt_bias, a_log, d_skip, ssd_norm,
           rel_bias, attn_norm, w_out, norm_mix, norm_ffn, router_group, router_expert, w_gate, w_up, w_down,
           norm_ple, w_ple_proj, w_ple_gate, norm_final):
    prm = _prepare(w_in, conv_w, conv_b, dt_bias, a_log, d_skip, ssd_norm, rel_bias, attn_norm, w_out, norm_mix,
                   norm_ffn, router_group, router_expert, w_gate, w_up, w_down, norm_ple, w_ple_proj, w_ple_gate,
                   norm_final)
    y_prompt = _encoder(x_prompt, p_prompt[0], prm)
    y_sample = _encoder(x_sample, p_sample[0], prm)
    return (y_prompt, y_sample)
```

```python
import functools
import math

import numpy as np
import jax
import jax.numpy as jnp
from jax import lax
from jax.experimental import pallas as pl
from jax.experimental.pallas import tpu as pltpu

F32 = jnp.float32
BF16 = jnp.bfloat16

D_MODEL = 1024
SSD_WIDTH = 1024
HEAD_DIM = 64
N_HEADS = 16
N_BC_GROUPS = 2
D_STATE = 128
D_CONV = 5
CHUNK = 128
CONV_DIM = SSD_WIDTH + 2 * N_BC_GROUPS * D_STATE
HEADS_PER_GROUP = N_HEADS // N_BC_GROUPS
GROUP_WIDTH = SSD_WIDTH // N_BC_GROUPS
GRID_W = 64
WIN_H = 8
WIN_W = 16
N_GROUPS = 4
EXPERTS_PER_GROUP = 8
N_EXPERTS = N_GROUPS * EXPERTS_PER_GROUP
D_EXPERT = 512
PLE_DIM = 256
EPS = 1e-6
LOG2E = math.log2(math.e)

LANES = 128
HALO_ROWS = 16
TOKEN_TILE = 512
SSD_TILE = 512
CONV_COLS = 512
SSD_TRIP_CHUNKS = 2
NA_ROWS = 8
NA_TRIP_ROWS = 4
SLOT_BLOCK = 512
DISPATCH_TILE = 1024
COMBINE_TILE = 512
ROW_GROUP = 8
ROW_SUB = 8
ROUTER_ROWS = 48
VMEM_LIMIT = 56 * 1024 * 1024


def _rms(x, g):
    ms = jnp.mean(x * x, axis=-1, keepdims=True)
    return x * lax.rsqrt(ms + EPS) * g


def _split3(x):
    hi = x.astype(BF16)
    r1 = x - hi.astype(F32)
    mid = r1.astype(BF16)
    lo = (r1 - mid.astype(F32)).astype(BF16)
    return hi, mid, lo


def _dot(a, b):
    return jnp.dot(a, b, preferred_element_type=F32)


def _dot_nt(a, b):
    return lax.dot_general(a, b, (((1,), (1,)), ((), ())), preferred_element_type=F32)


def _tiled(m):
    return (m * ROW_SUB, LANES)


def _dot_tn(a, b):
    return lax.dot_general(a, b, (((0,), (0,)), ((), ())), preferred_element_type=F32)


def _store_row_tiles(ref, x):
    m = x.shape[0]
    for s in range(ROW_SUB):
        ref[pl.ds(s, m, stride=ROW_SUB), :] = x[:, s * LANES:(s + 1) * LANES]


def _load_row_tiles(ref):
    m = ref.shape[0] // ROW_SUB
    return jnp.concatenate([ref[pl.ds(s, m, stride=ROW_SUB), :] for s in range(ROW_SUB)], axis=1)


def _params(sem):
    return pltpu.CompilerParams(dimension_semantics=sem, vmem_limit_bytes=VMEM_LIMIT)


def _inproj_kernel(x_ref, g_ref, wz_ref, wxbc_ref, wq_ref, wk_ref, wv_ref, wdt_ref,
                   z_ref, xbc_ref, q_ref, k_ref, v_ref, dt_ref):
    u = _rms(x_ref[...], g_ref[...]).astype(BF16)
    z_ref[...] = _dot(u, wz_ref[...]).astype(BF16)
    xbc_ref[...] = _dot(u, wxbc_ref[...]).astype(BF16)
    q_ref[...] = (_dot(u, wq_ref[...]) * (HEAD_DIM ** -0.5 * LOG2E)).astype(BF16)
    k_ref[...] = _dot(u, wk_ref[...]).astype(BF16)
    v_ref[...] = _dot(u, wv_ref[...]).astype(BF16)
    dt_ref[...] = _dot(u, wdt_ref[...])


def _inproj(x, g, wz, wxbc, wq, wk, wv, wdt):
    n = x.shape[0]
    tm = TOKEN_TILE
    row = lambda w: pl.BlockSpec((tm, w), lambda i: (i, 0))
    full = lambda a: pl.BlockSpec(a.shape, lambda i: (0,) * a.ndim)
    outs = [(D_MODEL, BF16), (CONV_DIM, BF16), (D_MODEL, BF16), (D_MODEL, BF16), (D_MODEL, BF16),
            (2 * LANES, F32)]
    return pl.pallas_call(
        _inproj_kernel,
        grid=(n // tm,),
        in_specs=[row(D_MODEL), full(g), full(wz), full(wxbc), full(wq), full(wk), full(wv), full(wdt)],
        out_specs=[row(w) for w, _ in outs],
        out_shape=[jax.ShapeDtypeStruct((n, w), d) for w, d in outs],
        compiler_params=_params(("parallel",)),
        name="inproj",
    )(x, g, wz, wxbc, wq, wk, wv, wdt)


def _ssd_kernel(*refs, backward, n_blocks):
    if backward:
        (xbc_ref, prev_ref, next_ref, dt_ref, cw_ref, cb_ref, dtb_ref, alog_ref,
         out_ref, act_ref, xp_ref, st_ref) = refs
    else:
        (act_ref, dt_ref, dtb_ref, alog_ref, z_ref, ybwd_ref, dexp_ref, gn_ref, out_ref, st_ref) = refs
    tb = act_ref.shape[0]
    t = CHUNK
    n_chunks = tb // t
    j = pl.program_id(1)

    @pl.when(j == 0)
    def _():
        st_ref[...] = jnp.zeros_like(st_ref)

    if backward:
        blk = n_blocks - 1 - j
        hr = HALO_ROWS
        xp_ref[0:hr, :] = jnp.where(blk == 0, jnp.zeros_like(prev_ref), prev_ref[...])
        xp_ref[hr:hr + tb, :] = xbc_ref[...]
        xp_ref[hr + tb:, :] = jnp.where(blk == n_blocks - 1, jnp.zeros_like(next_ref), next_ref[...])
        taps = [kk for kk in range(D_CONV) if kk != D_CONV // 2]
        win = t + 2 * hr
        srow = lax.broadcasted_iota(jnp.int32, (len(taps) * t, win), 0)
        scol = lax.broadcasted_iota(jnp.int32, (len(taps) * t, win), 1)
        tap_id = srow // t
        tap_off = tap_id + jnp.where(tap_id >= D_CONV // 2, 1, 0) + (hr - D_CONV // 2)
        shift_sel = jnp.where(scol == (srow % t) + tap_off, 1.0, 0.0).astype(BF16)
        for c in range(n_chunks):
            for c0 in range(0, CONV_DIM, CONV_COLS):
                cols = slice(c0, c0 + CONV_COLS)
                window = xp_ref[c * t:c * t + win, cols]
                shifted = _dot(shift_sel, window)
                acc = cb_ref[:, cols] + cw_ref[D_CONV // 2:D_CONV // 2 + 1, cols] * window[hr:hr + t].astype(F32)
                for ti, kk in enumerate(taps):
                    acc = acc + cw_ref[kk:kk + 1, cols] * shifted[ti * t:(ti + 1) * t]
                act_ref[c * t:(c + 1) * t, cols] = (acc * (1.0 / (1.0 + jnp.exp(-acc)))).astype(act_ref.dtype)

    a_row = -jnp.exp(alog_ref[...]) * LOG2E
    dtb = dtb_ref[...]
    ri = lax.broadcasted_iota(jnp.int32, (t, t), 0)
    ci = lax.broadcasted_iota(jnp.int32, (t, t), 1)
    if backward:
        keep = ci >= ri
    else:
        keep = ri >= ci
    tri = jnp.where(keep, 1.0, 0.0).astype(BF16)
    er = lax.broadcasted_iota(jnp.int32, (2 * LANES, SSD_WIDTH), 0)
    ec = lax.broadcasted_iota(jnp.int32, (2 * LANES, SSD_WIDTH), 1)
    expand = jnp.where((er % LANES) == (ec // HEAD_DIM), 1.0, 0.0).astype(BF16)
    lane = lax.broadcasted_iota(jnp.int32, (t, LANES), 1)
    low_half = lane < HEAD_DIM

    def chunk(c, st):
        r0 = pl.multiple_of(c * t, t)
        act = act_ref[pl.ds(r0, t), :]
        xs_bf = act[:, :SSD_WIDTH]
        xs = xs_bf.astype(F32)
        groups = []
        for g in range(N_BC_GROUPS):
            b_g = act[:, SSD_WIDTH + g * D_STATE:SSD_WIDTH + (g + 1) * D_STATE]
            c_g = act[:, SSD_WIDTH + (N_BC_GROUPS + g) * D_STATE:SSD_WIDTH + (N_BC_GROUPS + g + 1) * D_STATE]
            groups.append((b_g, c_g, slice(g * GROUP_WIDTH, (g + 1) * GROUP_WIDTH)))

        st_bf = st.astype(BF16)
        cbs = [_dot_nt(c_g, b_g) for b_g, c_g, _ in groups]
        y_offs = [_dot(c_g, st_bf[:, hs]) for _, c_g, hs in groups]

        dt_raw = dt_ref[pl.ds(r0, t), :] + dtb
        dt = jnp.maximum(dt_raw, 0.0) + jnp.log(1.0 + jnp.exp(-jnp.abs(dt_raw)))
        hi, mid, lo = _split3(dt * a_row)
        cs = _dot(tri, hi) + _dot(tri, mid) + _dot(tri, lo)
        cs_row = cs.T
        dt_row = dt.T
        edge = cs[0:1, :] if backward else cs[t - 1:t, :]
        w_in = dt * jnp.exp2(edge - cs)
        ecs = jnp.exp2(cs)
        both = jnp.concatenate([w_in, ecs], axis=0)
        b_hi = both.astype(BF16)
        b_lo = (both - b_hi.astype(F32)).astype(BF16)
        both_x = _dot(jnp.concatenate([b_hi, b_lo], axis=1), expand)
        w_x = both_x[:t]
        ecs_x = both_x[t:]
        cd_x = ecs_x[0:1, :] if backward else ecs_x[t - 1:t, :]

        y_parts = []
        for g, (_, _, hs) in enumerate(groups):
            pair_parts = []
            for pp in range(HEADS_PER_GROUP // 2):
                h0 = g * HEADS_PER_GROUP + 2 * pp
                xs_pair = xs_bf[:, h0 * HEAD_DIM:(h0 + 2) * HEAD_DIM]
                ys = []
                for h in (h0, h0 + 1):
                    diff = cs[:, h:h + 1] - cs_row[h:h + 1, :]
                    seg = jnp.exp2(jnp.where(keep, diff, -jnp.inf))
                    m = (cbs[g] * seg * dt_row[h:h + 1, :]).astype(BF16)
                    ys.append(_dot(m, xs_pair))
                pair_parts.append(jnp.where(low_half, ys[0], ys[1]))
            y_parts.append(jnp.concatenate(pair_parts, axis=1) + y_offs[g] * ecs_x[:, hs])
        y = jnp.concatenate(y_parts, axis=1)

        xw = (xs * w_x).astype(BF16)
        news = [_dot_tn(b_g, xw[:, hs]) for b_g, _, hs in groups]
        st_new = st * cd_x + jnp.concatenate(news, axis=1)

        if backward:
            out_ref[pl.ds(r0, t), :] = y.astype(out_ref.dtype)
        else:
            y = y + ybwd_ref[pl.ds(r0, t), :].astype(F32) + dexp_ref[...] * xs
            zz = z_ref[pl.ds(r0, t), :].astype(F32)
            y = y * (zz * (1.0 / (1.0 + jnp.exp(-zz))))
            out_ref[pl.ds(r0, t), :] = _rms(y, gn_ref[...]).astype(out_ref.dtype)
        return st_new

    def trip(i, carry):
        st = st_ref[...]
        for u in range(SSD_TRIP_CHUNKS):
            ci = i * SSD_TRIP_CHUNKS + u
            st = chunk((n_chunks - 1 - ci) if backward else ci, st)
        st_ref[...] = st
        return carry

    lax.fori_loop(0, n_chunks // SSD_TRIP_CHUNKS, trip, 0)


def _ssd_pass(xin, dt, dtb, alog, *, batch, seq, backward, extra=()):
    n = xin.shape[0]
    tb = SSD_TILE
    nb = seq // tb
    hb = tb // HALO_ROWS
    n_halo = n // HALO_ROWS
    d = 1 if backward else 0

    def blk_of(b, j):
        return b * nb + ((nb - 1 - j) if backward else j)

    main = lambda w: pl.BlockSpec((tb, w), lambda b, j: (blk_of(b, j), 0))
    full = lambda a: pl.BlockSpec(a.shape, lambda b, j: (0,) * a.ndim)
    dt_spec = pl.BlockSpec((tb, LANES), lambda b, j: (blk_of(b, j), d))
    state = pltpu.VMEM((D_STATE, SSD_WIDTH), F32)
    y_shape = jax.ShapeDtypeStruct((n, SSD_WIDTH), BF16)
    if backward:
        cw, cb = extra
        in_specs = [
            main(CONV_DIM),
            pl.BlockSpec((HALO_ROWS, CONV_DIM), lambda b, j: (jnp.maximum(blk_of(b, j) * hb - 1, 0), 0)),
            pl.BlockSpec((HALO_ROWS, CONV_DIM), lambda b, j: (jnp.minimum((blk_of(b, j) + 1) * hb, n_halo - 1), 0)),
            dt_spec, full(cw), full(cb), full(dtb), full(alog),
        ]
        args = [xin, xin, xin, dt, cw, cb, dtb, alog]
        out_specs = [main(SSD_WIDTH), main(CONV_DIM)]
        out_shape = [y_shape, jax.ShapeDtypeStruct((n, CONV_DIM), BF16)]
        scratch = [pltpu.VMEM((tb + 2 * HALO_ROWS, CONV_DIM), BF16), state]
    else:
        z, ybwd, dexp, gn = extra
        in_specs = [main(CONV_DIM), dt_spec, full(dtb), full(alog), main(SSD_WIDTH), main(SSD_WIDTH), full(dexp),
                    full(gn)]
        args = [xin, dt, dtb, alog, z, ybwd, dexp, gn]
        out_specs = main(SSD_WIDTH)
        out_shape = y_shape
        scratch = [state]
    return pl.pallas_call(
        functools.partial(_ssd_kernel, backward=backward, n_blocks=nb),
        grid=(batch, nb),
        in_specs=in_specs,
        out_specs=out_specs,
        out_shape=out_shape,
        scratch_shapes=scratch,
        compiler_params=_params(("parallel", "arbitrary")),
        name="ssd_bwd" if backward else "ssd_fwd",
    )(*args)


def _natten_kernel(q_ref, kp_ref, kc_ref, kn_ref, vp_ref, vc_ref, vn_ref, bias_ref, gn_ref,
                   out_ref, kbuf, vbuf, obuf, *, rows):
    half = NA_ROWS // 2 * GRID_W
    full = NA_ROWS * GRID_W
    jb = pl.program_id(1)
    kbuf[0:half, :] = kp_ref[...]
    kbuf[half:half + full, :] = kc_ref[...]
    kbuf[half + full:, :] = kn_ref[...]
    vbuf[0:half, :] = vp_ref[...]
    vbuf[half:half + full, :] = vc_ref[...]
    vbuf[half + full:, :] = vn_ref[...]
    lane = lax.broadcasted_iota(jnp.int32, (GRID_W, LANES), 1)
    low_half = lane < HEAD_DIM
    n_keys = WIN_H * GRID_W

    n_pairs = N_HEADS // 2
    lanes_of = lambda pp: slice(pp * LANES, (pp + 1) * LANES)

    def trip_body(it, carry):
        geo = []
        for rr in range(NA_TRIP_ROWS):
            i = it * NA_TRIP_ROWS + rr
            r = jb * NA_ROWS + i
            s = jnp.clip(r - WIN_H // 2, 0, rows - WIN_H)
            local = s - (jb * NA_ROWS - NA_ROWS // 2)
            geo.append((pl.multiple_of(local * GRID_W, GRID_W), pl.multiple_of(i * GRID_W, GRID_W),
                        s - r + (WIN_H - 1)))
        scores = []
        for k0, q0, d0 in geo:
            for pp in range(n_pairs):
                q_pair = q_ref[pl.ds(q0, GRID_W), lanes_of(pp)]
                zero = jnp.zeros_like(q_pair)
                q2 = jnp.concatenate([jnp.where(low_half, q_pair, zero), jnp.where(low_half, zero, q_pair)], axis=0)
                sc = _dot_nt(q2, kbuf[pl.ds(k0, n_keys), lanes_of(pp)])
                bias = jnp.concatenate(
                    [jnp.concatenate([bias_ref[d0 + 2 * ii, 2 * pp + hh] for ii in range(WIN_H // 2)], axis=1)
                     for hh in range(2)], axis=0)
                scores.append(sc + bias)
        probs = []
        for sc in scores:
            m = jnp.max(sc, axis=-1, keepdims=True)
            e = jnp.exp2(sc - m)
            probs.append((e.astype(BF16), 1.0 / jnp.sum(e, axis=-1, keepdims=True)))
        for rr, (k0, q0, d0) in enumerate(geo):
            for pp in range(n_pairs):
                e, inv = probs[rr * n_pairs + pp]
                o2 = _dot(e, vbuf[pl.ds(k0, n_keys), lanes_of(pp)]) * inv
                obuf[rr * GRID_W:(rr + 1) * GRID_W, lanes_of(pp)] = jnp.where(low_half, o2[:GRID_W], o2[GRID_W:])
        for rr, (k0, q0, d0) in enumerate(geo):
            o = obuf[rr * GRID_W:(rr + 1) * GRID_W, :]
            out_ref[pl.ds(q0, GRID_W), :] = _rms(o, gn_ref[...]).astype(out_ref.dtype)
        return carry

    lax.fori_loop(0, NA_ROWS // NA_TRIP_ROWS, trip_body, 0)


def _natten(q, k, v, bias_tab, gn, *, batch, seq):
    n = q.shape[0]
    rows = seq // GRID_W
    nrb = rows // NA_ROWS
    full = NA_ROWS * GRID_W
    half = full // 2
    n_half = n // half
    cur = pl.BlockSpec((full, D_MODEL), lambda b, j: (b * nrb + j, 0))
    prev = pl.BlockSpec((half, D_MODEL), lambda b, j: (jnp.maximum((b * nrb + j) * 2 - 1, 0), 0))
    nxt = pl.BlockSpec((half, D_MODEL), lambda b, j: (jnp.minimum((b * nrb + j) * 2 + 2, n_half - 1), 0))
    const = lambda a: pl.BlockSpec(a.shape, lambda b, j: (0,) * a.ndim)
    return pl.pallas_call(
        functools.partial(_natten_kernel, rows=rows),
        grid=(batch, nrb),
        in_specs=[cur, prev, cur, nxt, prev, cur, nxt, const(bias_tab), const(gn)],
        out_specs=cur,
        out_shape=jax.ShapeDtypeStruct((n, D_MODEL), BF16),
        scratch_shapes=[pltpu.VMEM((2 * full, D_MODEL), BF16), pltpu.VMEM((2 * full, D_MODEL), BF16),
                        pltpu.VMEM((NA_TRIP_ROWS * GRID_W, D_MODEL), F32)],
        compiler_params=_params(("parallel", "parallel")),
        name="natten",
    )(q, k, k, k, v, v, v, bias_tab, gn)


def _bias_table(rel_bias):
    qc = np.arange(GRID_W)[:, None]
    kc = np.arange(GRID_W)[None, :]
    s_c = np.clip(qc - WIN_W // 2, 0, GRID_W - WIN_W)
    inside = (kc >= s_c) & (kc < s_c + WIN_W)
    dcol = np.clip(kc - qc, -(WIN_W - 1), WIN_W - 1) + WIN_W - 1
    tab = rel_bias.astype(F32)[:, :, dcol]
    tab = jnp.where(inside[None, None], tab * LOG2E, -jnp.inf)
    pairs = jnp.concatenate([tab[:, :-1], tab[:, 1:]], axis=-1)
    return pairs.transpose(1, 0, 2, 3)


def _outproj_kernel(x_ref, ys_ref, ya_ref, wa_ref, wb_ref, gn_ref, wr_ref, upper_ref,
                    h_ref, u_ref, idx_ref, gate_ref, cnt_ref, carry_ref):
    tm = x_ref.shape[0]

    @pl.when(pl.program_id(0) == 0)
    def _():
        carry_ref[...] = jnp.zeros_like(carry_ref)

    h = x_ref[...] + _dot(ys_ref[...], wa_ref[...]) + _dot(ya_ref[...], wb_ref[...])
    h_ref[...] = h
    u = _rms(h, gn_ref[...])
    _store_row_tiles(u_ref, u)

    u_hi = u.astype(BF16)
    u_lo = (u - u_hi.astype(F32)).astype(BF16)
    wr = wr_ref[...]
    w_hi = wr.astype(BF16)
    w_lo = (wr - w_hi.astype(F32)).astype(BF16)
    logit = _dot_nt(w_hi, u_hi) + _dot_nt(w_hi, u_lo) + _dot_nt(w_lo, u_hi)

    lg = logit[0:N_GROUPS]
    gi = lax.broadcasted_iota(jnp.int32, (N_GROUPS, tm), 0)
    g_max = jnp.max(lg, axis=0, keepdims=True)
    g_idx = jnp.min(jnp.where(lg == g_max, gi, N_GROUPS), axis=0, keepdims=True)
    g_val = 1.0 / jnp.sum(jnp.exp(lg - g_max), axis=0, keepdims=True)

    le = logit[8:8 + EXPERTS_PER_GROUP]
    for g in range(1, N_GROUPS):
        le = jnp.where(g_idx == g, logit[8 + g * EXPERTS_PER_GROUP:8 + (g + 1) * EXPERTS_PER_GROUP], le)
    ei = lax.broadcasted_iota(jnp.int32, (EXPERTS_PER_GROUP, tm), 0)
    m1 = jnp.max(le, axis=0, keepdims=True)
    i1 = jnp.min(jnp.where(le == m1, ei, EXPERTS_PER_GROUP), axis=0, keepdims=True)
    le2 = jnp.where(ei == i1, -jnp.inf, le)
    m2 = jnp.max(le2, axis=0, keepdims=True)
    i2 = jnp.min(jnp.where(le2 == m2, ei, EXPERTS_PER_GROUP), axis=0, keepdims=True)
    e21 = jnp.exp(m2 - m1)
    den = 1.0 / (1.0 + e21)
    gate1 = g_val * den
    gate2 = g_val * e21 * den
    e1 = g_idx * EXPERTS_PER_GROUP + i1
    e2 = g_idx * EXPERTS_PER_GROUP + i2

    xi = lax.broadcasted_iota(jnp.int32, (N_EXPERTS, tm), 0)
    hot1 = xi == e1
    hot2 = xi == e2
    hot = jnp.where(jnp.logical_or(hot1, hot2), 1.0, 0.0)
    before = _dot(hot.astype(BF16), upper_ref[...])
    carry = carry_ref[...]
    before = before + jnp.tile(carry, (1, tm // LANES))
    rank1 = jnp.sum(jnp.where(hot1, before, 0.0), axis=0, keepdims=True)
    rank2 = jnp.sum(jnp.where(hot2, before, 0.0), axis=0, keepdims=True)
    carry = carry + jnp.sum(hot, axis=1, keepdims=True)
    carry_ref[...] = carry
    cnt_ref[...] = carry

    zi = jnp.zeros((4, tm), jnp.int32)
    idx_ref[...] = jnp.concatenate([e1, e2, rank1.astype(jnp.int32), rank2.astype(jnp.int32), zi], axis=0)
    gates = jnp.concatenate([gate1, gate2, jnp.zeros((LANES - 2, tm), F32)], axis=0)
    gate_ref[...] = gates.T


def _outproj(x, y_ssd, y_att, wa, wb, gn, wr, upper):
    n = x.shape[0]
    tm = TOKEN_TILE
    row = lambda w: pl.BlockSpec((tm, w), lambda i: (i, 0))
    full = lambda a: pl.BlockSpec(a.shape, lambda i: (0,) * a.ndim)
    return pl.pallas_call(
        _outproj_kernel,
        grid=(n // tm,),
        in_specs=[row(D_MODEL), row(D_MODEL), row(D_MODEL), full(wa), full(wb), full(gn), full(wr), full(upper)],
        out_specs=[row(D_MODEL), pl.BlockSpec(_tiled(tm), lambda i: (i, 0)),
                   pl.BlockSpec((8, tm), lambda i: (0, i)), row(LANES),
                   pl.BlockSpec((N_EXPERTS, LANES), lambda i: (0, 0))],
        out_shape=[jax.ShapeDtypeStruct((n, D_MODEL), F32), jax.ShapeDtypeStruct(_tiled(n), F32),
                   jax.ShapeDtypeStruct((8, n), jnp.int32), jax.ShapeDtypeStruct((n, LANES), F32),
                   jax.ShapeDtypeStruct((N_EXPERTS, LANES), F32)],
        scratch_shapes=[pltpu.VMEM((N_EXPERTS, LANES), F32)],
        compiler_params=_params(("arbitrary",)),
        name="outproj_router",
    )(x, y_ssd, y_att, wa, wb, gn, wr, upper)


def _rows(ref, first, count):
    return ref.at[pl.ds(pl.multiple_of(first * ROW_SUB, ROW_SUB), count * ROW_SUB)]


def _row_copy(src_ref, src_row, dst_ref, dst_row, sem):
    return pltpu.make_async_copy(_rows(src_ref, src_row, 1), _rows(dst_ref, dst_row, 1), sem)


def _for_each_choice(dest_ref, n_rows, fn, unrolled=False):
    if unrolled:
        for r in range(n_rows):
            for kk in range(2):
                fn(r, kk, dest_ref[0, 0, kk * n_rows + r])
        return

    def group(g, carry):
        base = pl.multiple_of(g * ROW_GROUP, ROW_GROUP)
        for jj in range(ROW_GROUP):
            for kk in range(2):
                fn(base + jj, kk, dest_ref[0, 0, kk * n_rows + base + jj])
        return carry

    lax.fori_loop(0, n_rows // ROW_GROUP, group, 0)


def _dispatch_kernel(pstart_ref, pend_ref, dest_ref, u_ref, slots_ref, zero_ref, sem):
    tm = u_ref.shape[0] // ROW_SUB

    @pl.when(pl.program_id(0) == 0)
    def _():
        zero_ref[...] = jnp.zeros_like(zero_ref)

        def fill(e):
            return pltpu.make_async_copy(zero_ref, _rows(slots_ref, pend_ref[e] - SLOT_BLOCK, SLOT_BLOCK), sem)

        for e in range(N_EXPERTS):
            @pl.when(pend_ref[e] > pstart_ref[e])
            def _():
                fill(e).start()
        for e in range(N_EXPERTS):
            @pl.when(pend_ref[e] > pstart_ref[e])
            def _():
                fill(e).wait()

        def tail(j):
            return pltpu.make_async_copy(zero_ref, _rows(slots_ref, j * SLOT_BLOCK, SLOT_BLOCK), sem)

        used = lax.div(pend_ref[N_EXPERTS - 1], SLOT_BLOCK)
        n_blocks = slots_ref.shape[0] // (SLOT_BLOCK * ROW_SUB)

        def tail_start(j, carry):
            tail(j).start()
            return carry

        def tail_wait(j, carry):
            tail(j).wait()
            return carry

        lax.fori_loop(used, n_blocks, tail_start, 0)
        lax.fori_loop(used, n_blocks, tail_wait, 0)

    _for_each_choice(dest_ref, tm, lambda r, kk, slot: _row_copy(u_ref, r, slots_ref, slot, sem).start(priority=kk))
    for _ in range(2):
        pltpu.make_async_copy(u_ref, _rows(slots_ref, 0, tm), sem).wait()


def _tile_slots(dest, tm):
    nt = dest.shape[1] // tm
    return dest.reshape(2, nt, tm).transpose(1, 0, 2).reshape(nt, 1, 2 * tm)


def _dispatch(u, dest, pad_start, pad_end, n_slots):
    n = u.shape[0] // ROW_SUB
    tm = DISPATCH_TILE
    dest = _tile_slots(dest, tm)
    grid_spec = pltpu.PrefetchScalarGridSpec(
        num_scalar_prefetch=2,
        grid=(n // tm,),
        in_specs=[pl.BlockSpec((1, 1, 2 * tm), lambda i, ps, pe: (i, 0, 0), memory_space=pltpu.SMEM),
                  pl.BlockSpec(_tiled(tm), lambda i, ps, pe: (i, 0))],
        out_specs=pl.BlockSpec(memory_space=pl.ANY),
        scratch_shapes=[pltpu.VMEM(_tiled(SLOT_BLOCK), F32), pltpu.SemaphoreType.DMA(())],
    )
    return pl.pallas_call(
        _dispatch_kernel,
        grid_spec=grid_spec,
        out_shape=jax.ShapeDtypeStruct(_tiled(n_slots), F32),
        compiler_params=_params(("arbitrary",)),
        name="dispatch",
    )(pad_start, pad_end, dest, u)


def _expert_kernel(blk_e_ref, used_ref, x_ref, wg_ref, wu_ref, wd_ref, y_ref):
    del blk_e_ref

    @pl.when(pl.program_id(0) < used_ref[0])
    def _():
        x = _load_row_tiles(x_ref).astype(BF16)
        a = _dot(x, wg_ref[0])
        b = _dot(x, wu_ref[0])
        hid = (a * (1.0 / (1.0 + jnp.exp(-a))) * b).astype(BF16)
        _store_row_tiles(y_ref, _dot(hid, wd_ref[0]))

    @pl.when(pl.program_id(0) >= used_ref[0])
    def _():
        y_ref[...] = jnp.zeros_like(y_ref)


def _experts(slots, blk_e, used, wg, wu, wd):
    n_slots = slots.shape[0] // ROW_SUB
    nb = n_slots // SLOT_BLOCK
    xmap = lambda i, be, us: (jnp.minimum(i, us[0] - 1), 0)
    wmap = lambda i, be, us: (be[i], 0, 0)
    grid_spec = pltpu.PrefetchScalarGridSpec(
        num_scalar_prefetch=2,
        grid=(nb,),
        in_specs=[pl.BlockSpec(_tiled(SLOT_BLOCK), xmap),
                  pl.BlockSpec((1, D_MODEL, D_EXPERT), wmap),
                  pl.BlockSpec((1, D_MODEL, D_EXPERT), wmap),
                  pl.BlockSpec((1, D_EXPERT, D_MODEL), wmap)],
        out_specs=pl.BlockSpec(_tiled(SLOT_BLOCK), lambda i, be, us: (i, 0)),
    )
    return pl.pallas_call(
        _expert_kernel,
        grid_spec=grid_spec,
        out_shape=jax.ShapeDtypeStruct(_tiled(n_slots), F32),
        compiler_params=_params(("arbitrary",)),
        name="experts",
    )(blk_e, used, slots, wg, wu, wd)


def _combine_kernel(dest_ref, dest_next_ref, h_ref, gate_ref, p_ref, wp_ref, wg_ref, gp_ref,
                    gf_ref, y_ref, out_ref, buf, sems):
    tm = h_ref.shape[0]
    i = pl.program_id(0)

    def gather(dref, which, unrolled):
        _for_each_choice(dref, tm, lambda r, kk, slot: _row_copy(y_ref, slot, buf.at[which, kk], r,
                                                                 sems.at[which]).start(priority=kk), unrolled)

    def wait_tile(which):
        for kk in range(2):
            pltpu.make_async_copy(_rows(y_ref, 0, tm), buf.at[which, kk], sems.at[which]).wait()

    @pl.when(i == 0)
    def _():
        gather(dest_ref, 0, False)

    cur = i % 2
    wait_tile(cur)
    gather(dest_next_ref, 1 - cur, True)

    gates = gate_ref[...]
    moe = gates[:, 0:1] * _load_row_tiles(buf.at[cur, 0]) + gates[:, 1:2] * _load_row_tiles(buf.at[cur, 1])
    h = h_ref[...] + moe
    u = _rms(h, gp_ref[...]).astype(BF16)
    gate = 1.0 / (1.0 + jnp.exp(-_dot(u, wg_ref[...])))
    h = h + _dot(p_ref[...].astype(BF16), wp_ref[...]) * gate
    out_ref[...] = _rms(h, gf_ref[...])

    @pl.when(i + 1 == pl.num_programs(0))
    def _():
        wait_tile(1 - cur)


def _combine(dest, h, gate_col, p, wp, wg, gp, gf, y_sorted):
    n = h.shape[0]
    tm = COMBINE_TILE
    nt = n // tm
    dest = _tile_slots(dest, tm)
    row = lambda w: pl.BlockSpec((tm, w), lambda i: (i, 0))
    full = lambda a: pl.BlockSpec(a.shape, lambda i: (0,) * a.ndim)
    return pl.pallas_call(
        _combine_kernel,
        grid=(nt,),
        in_specs=[pl.BlockSpec((1, 1, 2 * tm), lambda i: (i, 0, 0), memory_space=pltpu.SMEM),
                  pl.BlockSpec((1, 1, 2 * tm), lambda i: (jnp.minimum(i + 1, nt - 1), 0, 0), memory_space=pltpu.SMEM),
                  row(D_MODEL), row(LANES), row(PLE_DIM), full(wp), full(wg), full(gp), full(gf),
                  pl.BlockSpec(memory_space=pl.ANY)],
        out_specs=row(D_MODEL),
        out_shape=jax.ShapeDtypeStruct((n, D_MODEL), F32),
        scratch_shapes=[pltpu.VMEM((2, 2) + _tiled(tm), F32), pltpu.SemaphoreType.DMA((2,))],
        compiler_params=_params(("arbitrary",)),
        name="combine_ple",
    )(dest, dest, h, gate_col, p, wp, wg, gp, gf, y_sorted)


def _pad_lanes(v, width=LANES):
    return jnp.zeros((1, width), F32).at[0, :v.shape[0]].set(v.astype(F32))


def _encoder(x3, p3, prm):
    batch, seq, _ = x3.shape
    n = batch * seq
    assert seq % SSD_TILE == 0 and (seq // GRID_W) % NA_ROWS == 0 and n % TOKEN_TILE == 0
    x = x3.reshape(n, D_MODEL)
    p = p3.reshape(n, PLE_DIM)

    z, xbc, q, k, v, dt = _inproj(x, prm["norm_mix"], prm["w_z"], prm["w_xbc"], prm["w_q"], prm["w_k"],
                                  prm["w_v"], prm["w_dt"])

    y_bwd, act = _ssd_pass(xbc, dt, prm["dt_bias"][1], prm["a_log"][1], batch=batch, seq=seq, backward=True,
                           extra=(prm["conv_w"], prm["conv_b"]))
    y_ssd = _ssd_pass(act, dt, prm["dt_bias"][0], prm["a_log"][0], batch=batch, seq=seq, backward=False,
                      extra=(z, y_bwd, prm["d_exp"], prm["ssd_norm"]))
    y_att = _natten(q, k, v, prm["bias_tab"], prm["attn_norm"], batch=batch, seq=seq)

    h1, u2, route, gate_col, counts = _outproj(x, y_ssd, y_att, prm["w_out_a"], prm["w_out_b"], prm["norm_ffn"],
                                               prm["w_router"], prm["upper"])

    cnt = counts[:, 0].astype(jnp.int32)
    padded = (cnt + SLOT_BLOCK - 1) // SLOT_BLOCK * SLOT_BLOCK
    pad_end = jnp.cumsum(padded)
    pad_start = pad_end - padded
    n_blocks = -(-(2 * n + N_EXPERTS * (SLOT_BLOCK - 1)) // SLOT_BLOCK)
    n_slots = n_blocks * SLOT_BLOCK
    first_slot = jnp.arange(n_blocks, dtype=jnp.int32) * SLOT_BLOCK
    blk_e = jnp.minimum(jnp.sum((pad_end[None, :] <= first_slot[:, None]).astype(jnp.int32), axis=1), N_EXPERTS - 1)
    used = pad_end[-1:] // SLOT_BLOCK
    hot = route[0:2][None] == jnp.arange(N_EXPERTS, dtype=jnp.int32)[:, None, None]
    dest = route[2:4] + jnp.sum(jnp.where(hot, pad_start[:, None, None], 0), axis=0)

    slots = _dispatch(u2, dest, pad_start, pad_end, n_slots)
    y_sorted = _experts(slots, blk_e, used, prm["w_gate"], prm["w_up"], prm["w_down"])
    out = _combine(dest, h1, gate_col, p, prm["w_ple_proj"], prm["w_ple_gate"], prm["norm_ple"],
                   prm["norm_final"], y_sorted)
    return out.reshape(batch, seq, D_MODEL)


def _prepare(w_in, conv_w, conv_b, dt_bias, a_log, d_skip, ssd_norm, rel_bias, attn_norm, w_out, norm_mix,
             norm_ffn, router_group, router_expert, w_gate, w_up, w_down, norm_ple, w_ple_proj, w_ple_gate,
             norm_final):
    w_in = w_in[0]
    offs = np.cumsum([0, SSD_WIDTH, CONV_DIM, 2 * N_HEADS, D_MODEL, D_MODEL, D_MODEL])
    piece = lambda i: w_in[:, offs[i]:offs[i + 1]]
    w_dt = jnp.zeros((D_MODEL, 2 * LANES), F32)
    w_dt = w_dt.at[:, :N_HEADS].set(piece(2)[:, :N_HEADS]).at[:, LANES:LANES + N_HEADS].set(piece(2)[:, N_HEADS:])
    row = lambda v: v.astype(F32).reshape(1, -1)
    w_router = jnp.zeros((ROUTER_ROWS, D_MODEL), F32)
    w_router = w_router.at[:N_GROUPS].set(router_group[0].T)
    w_router = w_router.at[8:8 + N_EXPERTS].set(router_expert[0].transpose(0, 2, 1).reshape(N_EXPERTS, D_MODEL))
    tm = TOKEN_TILE
    upper = jnp.asarray(np.triu(np.ones((tm, tm), np.float32), k=1), BF16)
    return {
        "norm_mix": row(norm_mix[0]),
        "w_z": piece(0).astype(BF16), "w_xbc": piece(1).astype(BF16), "w_dt": w_dt.astype(BF16),
        "w_q": piece(3).astype(BF16), "w_k": piece(4).astype(BF16), "w_v": piece(5).astype(BF16),
        "conv_w": jnp.zeros((8, CONV_DIM), F32).at[:D_CONV].set(conv_w[0]),
        "conv_b": row(conv_b[0]),
        "dt_bias": [_pad_lanes(dt_bias[0, d]) for d in range(2)],
        "a_log": [jnp.full((1, LANES), -jnp.inf, F32).at[0, :N_HEADS].set(a_log[0, d]) for d in range(2)],
        "d_exp": jnp.repeat(d_skip[0].astype(F32), HEAD_DIM).reshape(1, SSD_WIDTH),
        "ssd_norm": row(ssd_norm[0]),
        "bias_tab": _bias_table(rel_bias[0]),
        "attn_norm": row(attn_norm[0]),
        "w_out_a": w_out[0, :SSD_WIDTH].astype(BF16), "w_out_b": w_out[0, SSD_WIDTH:].astype(BF16),
        "norm_ffn": row(norm_ffn[0]),
        "w_router": w_router,
        "upper": upper,
        "w_gate": w_gate[0].astype(BF16), "w_up": w_up[0].astype(BF16), "w_down": w_down[0].astype(BF16),
        "norm_ple": row(norm_ple[0]),
        "w_ple_proj": w_ple_proj[0].astype(BF16), "w_ple_gate": w_ple_gate[0].astype(BF16),
        "norm_final": row(norm_final),
    }


def kernel(x_prompt, x_sample, p_prompt, p_sample, w_in, conv_w, conv_b, dt_bias, a_log, d_skip, ssd_norm,
           rel_bias, attn_norm, w_out, norm_mix, norm_ffn, router_group, router_expert, w_gate, w_up, w_down,
           norm_ple, w_ple_proj, w_ple_gate, norm_final):
    prm = _prepare(w_in, conv_w, conv_b, dt_bias, a_log, d_skip, ssd_norm, rel_bias, attn_norm, w_out, norm_mix,
                   norm_ffn, router_group, router_expert, w_gate, w_up, w_down, norm_ple, w_ple_proj, w_ple_gate,
                   norm_final)
    y_prompt = _encoder(x_prompt, p_prompt[0], prm)
    y_sample = _encoder(x_sample, p_sample[0], prm)
    return (y_prompt, y_sample)
```

```python
import functools
import math

import numpy as np
import jax
import jax.numpy as jnp
from jax import lax
from jax.experimental import pallas as pl
from jax.experimental.pallas import tpu as pltpu

F32 = jnp.float32
BF16 = jnp.bfloat16

D_MODEL = 1024
SSD_WIDTH = 1024
HEAD_DIM = 64
N_HEADS = 16
N_BC_GROUPS = 2
D_STATE = 128
D_CONV = 5
CHUNK = 128
CONV_DIM = SSD_WIDTH + 2 * N_BC_GROUPS * D_STATE
HEADS_PER_GROUP = N_HEADS // N_BC_GROUPS
GROUP_WIDTH = SSD_WIDTH // N_BC_GROUPS
GRID_W = 64
WIN_H = 8
WIN_W = 16
N_GROUPS = 4
EXPERTS_PER_GROUP = 8
N_EXPERTS = N_GROUPS * EXPERTS_PER_GROUP
D_EXPERT = 512
PLE_DIM = 256
EPS = 1e-6
LOG2E = math.log2(math.e)

LANES = 128
HALO_ROWS = 16
TOKEN_TILE = 512
SSD_TILE = 512
CONV_COLS = 512
SSD_TRIP_CHUNKS = 4
NA_ROWS = 8
NA_TRIP_ROWS = 8
SLOT_BLOCK = 512
DISPATCH_TILE = 1024
COMBINE_TILE = 512
ROW_GROUP = 8
ROW_SUB = 8
ROUTER_ROWS = 48
VMEM_LIMIT = 56 * 1024 * 1024


def _rms(x, g):
    ms = jnp.mean(x * x, axis=-1, keepdims=True)
    return x * lax.rsqrt(ms + EPS) * g


def _split3(x):
    hi = x.astype(BF16)
    r1 = x - hi.astype(F32)
    mid = r1.astype(BF16)
    lo = (r1 - mid.astype(F32)).astype(BF16)
    return hi, mid, lo


def _dot(a, b):
    return jnp.dot(a, b, preferred_element_type=F32)


def _dot_nt(a, b):
    return lax.dot_general(a, b, (((1,), (1,)), ((), ())), preferred_element_type=F32)


def _tiled(m):
    return (m * ROW_SUB, LANES)


def _dot_tn(a, b):
    return lax.dot_general(a, b, (((0,), (0,)), ((), ())), preferred_element_type=F32)


def _store_row_tiles(ref, x):
    m = x.shape[0]
    for s in range(ROW_SUB):
        ref[pl.ds(s, m, stride=ROW_SUB), :] = x[:, s * LANES:(s + 1) * LANES]


def _load_row_tiles(ref):
    m = ref.shape[0] // ROW_SUB
    return jnp.concatenate([ref[pl.ds(s, m, stride=ROW_SUB), :] for s in range(ROW_SUB)], axis=1)


def _params(sem):
    return pltpu.CompilerParams(dimension_semantics=sem, vmem_limit_bytes=VMEM_LIMIT)


def _inproj_kernel(x_ref, g_ref, wz_ref, wxbc_ref, wq_ref, wk_ref, wv_ref, wdt_ref,
                   z_ref, xbc_ref, q_ref, k_ref, v_ref, dt_ref):
    u = _rms(x_ref[...], g_ref[...]).astype(BF16)
    z_ref[...] = _dot(u, wz_ref[...]).astype(BF16)
    xbc_ref[...] = _dot(u, wxbc_ref[...]).astype(BF16)
    q_ref[...] = (_dot(u, wq_ref[...]) * (HEAD_DIM ** -0.5 * LOG2E)).astype(BF16)
    k_ref[...] = _dot(u, wk_ref[...]).astype(BF16)
    v_ref[...] = _dot(u, wv_ref[...]).astype(BF16)
    dt_ref[...] = _dot(u, wdt_ref[...])


def _inproj(x, g, wz, wxbc, wq, wk, wv, wdt):
    n = x.shape[0]
    tm = TOKEN_TILE
    row = lambda w: pl.BlockSpec((tm, w), lambda i: (i, 0))
    full = lambda a: pl.BlockSpec(a.shape, lambda i: (0,) * a.ndim)
    outs = [(D_MODEL, BF16), (CONV_DIM, BF16), (D_MODEL, BF16), (D_MODEL, BF16), (D_MODEL, BF16),
            (2 * LANES, F32)]
    return pl.pallas_call(
        _inproj_kernel,
        grid=(n // tm,),
        in_specs=[row(D_MODEL), full(g), full(wz), full(wxbc), full(wq), full(wk), full(wv), full(wdt)],
        out_specs=[row(w) for w, _ in outs],
        out_shape=[jax.ShapeDtypeStruct((n, w), d) for w, d in outs],
        compiler_params=_params(("parallel",)),
        name="inproj",
    )(x, g, wz, wxbc, wq, wk, wv, wdt)


def _ssd_kernel(*refs, backward, n_blocks):
    if backward:
        (xbc_ref, prev_ref, next_ref, dt_ref, cw_ref, cb_ref, dtb_ref, alog_ref,
         out_ref, act_ref, xp_ref, st_ref) = refs
    else:
        (act_ref, dt_ref, dtb_ref, alog_ref, z_ref, ybwd_ref, dexp_ref, gn_ref, out_ref, st_ref) = refs
    tb = act_ref.shape[0]
    t = CHUNK
    n_chunks = tb // t
    j = pl.program_id(1)

    @pl.when(j == 0)
    def _():
        st_ref[...] = jnp.zeros_like(st_ref)

    if backward:
        blk = n_blocks - 1 - j
        hr = HALO_ROWS
        xp_ref[0:hr, :] = jnp.where(blk == 0, jnp.zeros_like(prev_ref), prev_ref[...])
        xp_ref[hr:hr + tb, :] = xbc_ref[...]
        xp_ref[hr + tb:, :] = jnp.where(blk == n_blocks - 1, jnp.zeros_like(next_ref), next_ref[...])
        taps = [kk for kk in range(D_CONV) if kk != D_CONV // 2]
        win = t + 2 * hr
        srow = lax.broadcasted_iota(jnp.int32, (len(taps) * t, win), 0)
        scol = lax.broadcasted_iota(jnp.int32, (len(taps) * t, win), 1)
        tap_id = srow // t
        tap_off = tap_id + jnp.where(tap_id >= D_CONV // 2, 1, 0) + (hr - D_CONV // 2)
        shift_sel = jnp.where(scol == (srow % t) + tap_off, 1.0, 0.0).astype(BF16)
        for c in range(n_chunks):
            for c0 in range(0, CONV_DIM, CONV_COLS):
                cols = slice(c0, c0 + CONV_COLS)
                window = xp_ref[c * t:c * t + win, cols]
                shifted = _dot(shift_sel, window)
                acc = cb_ref[:, cols] + cw_ref[D_CONV // 2:D_CONV // 2 + 1, cols] * window[hr:hr + t].astype(F32)
                for ti, kk in enumerate(taps):
                    acc = acc + cw_ref[kk:kk + 1, cols] * shifted[ti * t:(ti + 1) * t]
                act_ref[c * t:(c + 1) * t, cols] = (acc * (1.0 / (1.0 + jnp.exp(-acc)))).astype(act_ref.dtype)

    a_row = -jnp.exp(alog_ref[...]) * LOG2E
    dtb = dtb_ref[...]
    ri = lax.broadcasted_iota(jnp.int32, (t, t), 0)
    ci = lax.broadcasted_iota(jnp.int32, (t, t), 1)
    if backward:
        keep = ci >= ri
    else:
        keep = ri >= ci
    tri = jnp.where(keep, 1.0, 0.0).astype(BF16)
    er = lax.broadcasted_iota(jnp.int32, (2 * LANES, SSD_WIDTH), 0)
    ec = lax.broadcasted_iota(jnp.int32, (2 * LANES, SSD_WIDTH), 1)
    expand = jnp.where((er % LANES) == (ec // HEAD_DIM), 1.0, 0.0).astype(BF16)
    lane = lax.broadcasted_iota(jnp.int32, (t, LANES), 1)
    low_half = lane < HEAD_DIM

    def chunk(c, st):
        r0 = pl.multiple_of(c * t, t)
        act = act_ref[pl.ds(r0, t), :]
        xs_bf = act[:, :SSD_WIDTH]
        xs = xs_bf.astype(F32)
        groups = []
        for g in range(N_BC_GROUPS):
            b_g = act[:, SSD_WIDTH + g * D_STATE:SSD_WIDTH + (g + 1) * D_STATE]
            c_g = act[:, SSD_WIDTH + (N_BC_GROUPS + g) * D_STATE:SSD_WIDTH + (N_BC_GROUPS + g + 1) * D_STATE]
            groups.append((b_g, c_g, slice(g * GROUP_WIDTH, (g + 1) * GROUP_WIDTH)))

        st_bf = st.astype(BF16)
        cbs = [_dot_nt(c_g, b_g) for b_g, c_g, _ in groups]
        y_offs = [_dot(c_g, st_bf[:, hs]) for _, c_g, hs in groups]

        dt_raw = dt_ref[pl.ds(r0, t), :] + dtb
        dt = jnp.maximum(dt_raw, 0.0) + jnp.log(1.0 + jnp.exp(-jnp.abs(dt_raw)))
        hi, mid, lo = _split3(dt * a_row)
        cs = _dot(tri, hi) + _dot(tri, mid) + _dot(tri, lo)
        cs_row = cs.T
        dt_row = dt.T
        edge = cs[0:1, :] if backward else cs[t - 1:t, :]
        w_in = dt * jnp.exp2(edge - cs)
        ecs = jnp.exp2(cs)
        both = jnp.concatenate([w_in, ecs], axis=0)
        b_hi = both.astype(BF16)
        b_lo = (both - b_hi.astype(F32)).astype(BF16)
        both_x = _dot(jnp.concatenate([b_hi, b_lo], axis=1), expand)
        w_x = both_x[:t]
        ecs_x = both_x[t:]
        cd_x = ecs_x[0:1, :] if backward else ecs_x[t - 1:t, :]

        y_parts = []
        for g, (_, _, hs) in enumerate(groups):
            pair_parts = []
            for pp in range(HEADS_PER_GROUP // 2):
                h0 = g * HEADS_PER_GROUP + 2 * pp
                xs_pair = xs_bf[:, h0 * HEAD_DIM:(h0 + 2) * HEAD_DIM]
                ys = []
                for h in (h0, h0 + 1):
                    diff = cs[:, h:h + 1] - cs_row[h:h + 1, :]
                    seg = jnp.exp2(jnp.where(keep, diff, -jnp.inf))
                    m = (cbs[g] * seg * dt_row[h:h + 1, :]).astype(BF16)
                    ys.append(_dot(m, xs_pair))
                pair_parts.append(jnp.where(low_half, ys[0], ys[1]))
            y_parts.append(jnp.concatenate(pair_parts, axis=1) + y_offs[g] * ecs_x[:, hs])
        y = jnp.concatenate(y_parts, axis=1)

        xw = (xs * w_x).astype(BF16)
        news = [_dot_tn(b_g, xw[:, hs]) for b_g, _, hs in groups]
        st_new = st * cd_x + jnp.concatenate(news, axis=1)

        if backward:
            out_ref[pl.ds(r0, t), :] = y.astype(out_ref.dtype)
        else:
            y = y + ybwd_ref[pl.ds(r0, t), :].astype(F32) + dexp_ref[...] * xs
            zz = z_ref[pl.ds(r0, t), :].astype(F32)
            y = y * (zz * (1.0 / (1.0 + jnp.exp(-zz))))
            out_ref[pl.ds(r0, t), :] = _rms(y, gn_ref[...]).astype(out_ref.dtype)
        return st_new

    def trip(i, carry):
        st = st_ref[...]
        for u in range(SSD_TRIP_CHUNKS):
            ci = i * SSD_TRIP_CHUNKS + u
            st = chunk((n_chunks - 1 - ci) if backward else ci, st)
        st_ref[...] = st
        return carry

    lax.fori_loop(0, n_chunks // SSD_TRIP_CHUNKS, trip, 0)


def _ssd_pass(xin, dt, dtb, alog, *, batch, seq, backward, extra=()):
    n = xin.shape[0]
    tb = SSD_TILE
    nb = seq // tb
    hb = tb // HALO_ROWS
    n_halo = n // HALO_ROWS
    d = 1 if backward else 0

    def blk_of(b, j):
        return b * nb + ((nb - 1 - j) if backward else j)

    main = lambda w: pl.BlockSpec((tb, w), lambda b, j: (blk_of(b, j), 0))
    full = lambda a: pl.BlockSpec(a.shape, lambda b, j: (0,) * a.ndim)
    dt_spec = pl.BlockSpec((tb, LANES), lambda b, j: (blk_of(b, j), d))
    state = pltpu.VMEM((D_STATE, SSD_WIDTH), F32)
    y_shape = jax.ShapeDtypeStruct((n, SSD_WIDTH), BF16)
    if backward:
        cw, cb = extra
        in_specs = [
            main(CONV_DIM),
            pl.BlockSpec((HALO_ROWS, CONV_DIM), lambda b, j: (jnp.maximum(blk_of(b, j) * hb - 1, 0), 0)),
            pl.BlockSpec((HALO_ROWS, CONV_DIM), lambda b, j: (jnp.minimum((blk_of(b, j) + 1) * hb, n_halo - 1), 0)),
            dt_spec, full(cw), full(cb), full(dtb), full(alog),
        ]
        args = [xin, xin, xin, dt, cw, cb, dtb, alog]
        out_specs = [main(SSD_WIDTH), main(CONV_DIM)]
        out_shape = [y_shape, jax.ShapeDtypeStruct((n, CONV_DIM), BF16)]
        scratch = [pltpu.VMEM((tb + 2 * HALO_ROWS, CONV_DIM), BF16), state]
    else:
        z, ybwd, dexp, gn = extra
        in_specs = [main(CONV_DIM), dt_spec, full(dtb), full(alog), main(SSD_WIDTH), main(SSD_WIDTH), full(dexp),
                    full(gn)]
        args = [xin, dt, dtb, alog, z, ybwd, dexp, gn]
        out_specs = main(SSD_WIDTH)
        out_shape = y_shape
        scratch = [state]
    return pl.pallas_call(
        functools.partial(_ssd_kernel, backward=backward, n_blocks=nb),
        grid=(batch, nb),
        in_specs=in_specs,
        out_specs=out_specs,
        out_shape=out_shape,
        scratch_shapes=scratch,
        compiler_params=_params(("parallel", "arbitrary")),
        name="ssd_bwd" if backward else "ssd_fwd",
    )(*args)


def _natten_kernel(q_ref, kp_ref, kc_ref, kn_ref, vp_ref, vc_ref, vn_ref, bias_ref, gn_ref,
                   out_ref, kbuf, vbuf, obuf, *, rows):
    half = NA_ROWS // 2 * GRID_W
    full = NA_ROWS * GRID_W
    jb = pl.program_id(1)
    kbuf[0:half, :] = kp_ref[...]
    kbuf[half:half + full, :] = kc_ref[...]
    kbuf[half + full:, :] = kn_ref[...]
    vbuf[0:half, :] = vp_ref[...]
    vbuf[half:half + full, :] = vc_ref[...]
    vbuf[half + full:, :] = vn_ref[...]
    lane = lax.broadcasted_iota(jnp.int32, (GRID_W, LANES), 1)
    low_half = lane < HEAD_DIM
    n_keys = WIN_H * GRID_W

    n_pairs = N_HEADS // 2
    lanes_of = lambda pp: slice(pp * LANES, (pp + 1) * LANES)

    def trip_body(it, carry):
        geo = []
        for rr in range(NA_TRIP_ROWS):
            i = it * NA_TRIP_ROWS + rr
            r = jb * NA_ROWS + i
            s = jnp.clip(r - WIN_H // 2, 0, rows - WIN_H)
            local = s - (jb * NA_ROWS - NA_ROWS // 2)
            geo.append((pl.multiple_of(local * GRID_W, GRID_W), pl.multiple_of(i * GRID_W, GRID_W),
                        s - r + (WIN_H - 1)))
        scores = []
        for k0, q0, d0 in geo:
            for pp in range(n_pairs):
                q_pair = q_ref[pl.ds(q0, GRID_W), lanes_of(pp)]
                zero = jnp.zeros_like(q_pair)
                q2 = jnp.concatenate([jnp.where(low_half, q_pair, zero), jnp.where(low_half, zero, q_pair)], axis=0)
                sc = _dot_nt(q2, kbuf[pl.ds(k0, n_keys), lanes_of(pp)])
                bias = jnp.concatenate(
                    [jnp.concatenate([bias_ref[d0 + 2 * ii, 2 * pp + hh] for ii in range(WIN_H // 2)], axis=1)
                     for hh in range(2)], axis=0)
                scores.append(sc + bias)
        probs = []
        for sc in scores:
            m = jnp.max(sc, axis=-1, keepdims=True)
            e = jnp.exp2(sc - m)
            probs.append((e.astype(BF16), 1.0 / jnp.sum(e, axis=-1, keepdims=True)))
        for rr, (k0, q0, d0) in enumerate(geo):
            for pp in range(n_pairs):
                e, inv = probs[rr * n_pairs + pp]
                o2 = _dot(e, vbuf[pl.ds(k0, n_keys), lanes_of(pp)]) * inv
                obuf[rr * GRID_W:(rr + 1) * GRID_W, lanes_of(pp)] = jnp.where(low_half, o2[:GRID_W], o2[GRID_W:])
        for rr, (k0, q0, d0) in enumerate(geo):
            o = obuf[rr * GRID_W:(rr + 1) * GRID_W, :]
            out_ref[pl.ds(q0, GRID_W), :] = _rms(o, gn_ref[...]).astype(out_ref.dtype)
        return carry

    lax.fori_loop(0, NA_ROWS // NA_TRIP_ROWS, trip_body, 0)


def _natten(q, k, v, bias_tab, gn, *, batch, seq):
    n = q.shape[0]
    rows = seq // GRID_W
    nrb = rows // NA_ROWS
    full = NA_ROWS * GRID_W
    half = full // 2
    n_half = n // half
    cur = pl.BlockSpec((full, D_MODEL), lambda b, j: (b * nrb + j, 0))
    prev = pl.BlockSpec((half, D_MODEL), lambda b, j: (jnp.maximum((b * nrb + j) * 2 - 1, 0), 0))
    nxt = pl.BlockSpec((half, D_MODEL), lambda b, j: (jnp.minimum((b * nrb + j) * 2 + 2, n_half - 1), 0))
    const = lambda a: pl.BlockSpec(a.shape, lambda b, j: (0,) * a.ndim)
    return pl.pallas_call(
        functools.partial(_natten_kernel, rows=rows),
        grid=(batch, nrb),
        in_specs=[cur, prev, cur, nxt, prev, cur, nxt, const(bias_tab), const(gn)],
        out_specs=cur,
        out_shape=jax.ShapeDtypeStruct((n, D_MODEL), BF16),
        scratch_shapes=[pltpu.VMEM((2 * full, D_MODEL), BF16), pltpu.VMEM((2 * full, D_MODEL), BF16),
                        pltpu.VMEM((NA_TRIP_ROWS * GRID_W, D_MODEL), F32)],
        compiler_params=_params(("parallel", "parallel")),
        name="natten",
    )(q, k, k, k, v, v, v, bias_tab, gn)


def _bias_table(rel_bias):
    qc = np.arange(GRID_W)[:, None]
    kc = np.arange(GRID_W)[None, :]
    s_c = np.clip(qc - WIN_W // 2, 0, GRID_W - WIN_W)
    inside = (kc >= s_c) & (kc < s_c + WIN_W)
    dcol = np.clip(kc - qc, -(WIN_W - 1), WIN_W - 1) + WIN_W - 1
    tab = rel_bias.astype(F32)[:, :, dcol]
    tab = jnp.where(inside[None, None], tab * LOG2E, -jnp.inf)
    pairs = jnp.concatenate([tab[:, :-1], tab[:, 1:]], axis=-1)
    return pairs.transpose(1, 0, 2, 3)


def _outproj_kernel(x_ref, ys_ref, ya_ref, wa_ref, wb_ref, gn_ref, wr_ref, upper_ref,
                    h_ref, u_ref, idx_ref, gate_ref, cnt_ref, carry_ref):
    tm = x_ref.shape[0]

    @pl.when(pl.program_id(0) == 0)
    def _():
        carry_ref[...] = jnp.zeros_like(carry_ref)

    h = x_ref[...] + _dot(ys_ref[...], wa_ref[...]) + _dot(ya_ref[...], wb_ref[...])
    h_ref[...] = h
    u = _rms(h, gn_ref[...])
    _store_row_tiles(u_ref, u)

    u_hi = u.astype(BF16)
    u_lo = (u - u_hi.astype(F32)).astype(BF16)
    wr = wr_ref[...]
    w_hi = wr.astype(BF16)
    w_lo = (wr - w_hi.astype(F32)).astype(BF16)
    both = _dot_nt(jnp.concatenate([w_hi, w_lo], axis=0), u_hi)
    logit = both[:ROUTER_ROWS] + both[ROUTER_ROWS:] + _dot_nt(w_hi, u_lo)

    lg = logit[0:N_GROUPS]
    gi = lax.broadcasted_iota(jnp.int32, (N_GROUPS, tm), 0)
    g_max = jnp.max(lg, axis=0, keepdims=True)
    g_idx = jnp.min(jnp.where(lg == g_max, gi, N_GROUPS), axis=0, keepdims=True)
    g_val = 1.0 / jnp.sum(jnp.exp(lg - g_max), axis=0, keepdims=True)

    le = logit[8:8 + EXPERTS_PER_GROUP]
    for g in range(1, N_GROUPS):
        le = jnp.where(g_idx == g, logit[8 + g * EXPERTS_PER_GROUP:8 + (g + 1) * EXPERTS_PER_GROUP], le)
    ei = lax.broadcasted_iota(jnp.int32, (EXPERTS_PER_GROUP, tm), 0)
    m1 = jnp.max(le, axis=0, keepdims=True)
    i1 = jnp.min(jnp.where(le == m1, ei, EXPERTS_PER_GROUP), axis=0, keepdims=True)
    le2 = jnp.where(ei == i1, -jnp.inf, le)
    m2 = jnp.max(le2, axis=0, keepdims=True)
    i2 = jnp.min(jnp.where(le2 == m2, ei, EXPERTS_PER_GROUP), axis=0, keepdims=True)
    e21 = jnp.exp(m2 - m1)
    den = 1.0 / (1.0 + e21)
    gate1 = g_val * den
    gate2 = g_val * e21 * den
    e1 = g_idx * EXPERTS_PER_GROUP + i1
    e2 = g_idx * EXPERTS_PER_GROUP + i2

    xi = lax.broadcasted_iota(jnp.int32, (N_EXPERTS, tm), 0)
    hot1 = xi == e1
    hot2 = xi == e2
    hot = jnp.where(jnp.logical_or(hot1, hot2), 1.0, 0.0)
    before = _dot(hot.astype(BF16), upper_ref[...])
    carry = carry_ref[...]
    before = before + jnp.tile(carry, (1, tm // LANES))
    rank1 = jnp.sum(jnp.where(hot1, before, 0.0), axis=0, keepdims=True)
    rank2 = jnp.sum(jnp.where(hot2, before, 0.0), axis=0, keepdims=True)
    carry = carry + jnp.sum(hot, axis=1, keepdims=True)
    carry_ref[...] = carry
    cnt_ref[...] = carry

    zi = jnp.zeros((4, tm), jnp.int32)
    idx_ref[...] = jnp.concatenate([e1, e2, rank1.astype(jnp.int32), rank2.astype(jnp.int32), zi], axis=0)
    gates = jnp.concatenate([gate1, gate2, jnp.zeros((LANES - 2, tm), F32)], axis=0)
    gate_ref[...] = gates.T


def _outproj(x, y_ssd, y_att, wa, wb, gn, wr, upper):
    n = x.shape[0]
    tm = TOKEN_TILE
    row = lambda w: pl.BlockSpec((tm, w), lambda i: (i, 0))
    full = lambda a: pl.BlockSpec(a.shape, lambda i: (0,) * a.ndim)
    return pl.pallas_call(
        _outproj_kernel,
        grid=(n // tm,),
        in_specs=[row(D_MODEL), row(D_MODEL), row(D_MODEL), full(wa), full(wb), full(gn), full(wr), full(upper)],
        out_specs=[row(D_MODEL), pl.BlockSpec(_tiled(tm), lambda i: (i, 0)),
                   pl.BlockSpec((8, tm), lambda i: (0, i)), row(LANES),
                   pl.BlockSpec((N_EXPERTS, LANES), lambda i: (0, 0))],
        out_shape=[jax.ShapeDtypeStruct((n, D_MODEL), F32), jax.ShapeDtypeStruct(_tiled(n), F32),
                   jax.ShapeDtypeStruct((8, n), jnp.int32), jax.ShapeDtypeStruct((n, LANES), F32),
                   jax.ShapeDtypeStruct((N_EXPERTS, LANES), F32)],
        scratch_shapes=[pltpu.VMEM((N_EXPERTS, LANES), F32)],
        compiler_params=_params(("arbitrary",)),
        name="outproj_router",
    )(x, y_ssd, y_att, wa, wb, gn, wr, upper)


def _rows(ref, first, count):
    return ref.at[pl.ds(pl.multiple_of(first * ROW_SUB, ROW_SUB), count * ROW_SUB)]


def _row_copy(src_ref, src_row, dst_ref, dst_row, sem):
    return pltpu.make_async_copy(_rows(src_ref, src_row, 1), _rows(dst_ref, dst_row, 1), sem)


def _for_each_choice(dest_ref, n_rows, fn, unrolled=False):
    if unrolled:
        for r in range(n_rows):
            for kk in range(2):
                fn(r, kk, dest_ref[0, 0, kk * n_rows + r])
        return

    def group(g, carry):
        base = pl.multiple_of(g * ROW_GROUP, ROW_GROUP)
        for jj in range(ROW_GROUP):
            for kk in range(2):
                fn(base + jj, kk, dest_ref[0, 0, kk * n_rows + base + jj])
        return carry

    lax.fori_loop(0, n_rows // ROW_GROUP, group, 0)


def _dispatch_kernel(pstart_ref, pend_ref, dest_ref, u_ref, slots_ref, zero_ref, sem):
    tm = u_ref.shape[0] // ROW_SUB

    @pl.when(pl.program_id(0) == 0)
    def _():
        zero_ref[...] = jnp.zeros_like(zero_ref)

        def fill(e):
            return pltpu.make_async_copy(zero_ref, _rows(slots_ref, pend_ref[e] - SLOT_BLOCK, SLOT_BLOCK), sem)

        for e in range(N_EXPERTS):
            @pl.when(pend_ref[e] > pstart_ref[e])
            def _():
                fill(e).start()
        for e in range(N_EXPERTS):
            @pl.when(pend_ref[e] > pstart_ref[e])
            def _():
                fill(e).wait()

        def tail(j):
            return pltpu.make_async_copy(zero_ref, _rows(slots_ref, j * SLOT_BLOCK, SLOT_BLOCK), sem)

        used = lax.div(pend_ref[N_EXPERTS - 1], SLOT_BLOCK)
        n_blocks = slots_ref.shape[0] // (SLOT_BLOCK * ROW_SUB)

        def tail_start(j, carry):
            tail(j).start()
            return carry

        def tail_wait(j, carry):
            tail(j).wait()
            return carry

        lax.fori_loop(used, n_blocks, tail_start, 0)
        lax.fori_loop(used, n_blocks, tail_wait, 0)

    _for_each_choice(dest_ref, tm, lambda r, kk, slot: _row_copy(u_ref, r, slots_ref, slot, sem).start(priority=kk))
    for _ in range(2):
        pltpu.make_async_copy(u_ref, _rows(slots_ref, 0, tm), sem).wait()


def _tile_slots(dest, tm):
    nt = dest.shape[1] // tm
    return dest.reshape(2, nt, tm).transpose(1, 0, 2).reshape(nt, 1, 2 * tm)


def _dispatch(u, dest, pad_start, pad_end, n_slots):
    n = u.shape[0] // ROW_SUB
    tm = DISPATCH_TILE
    dest = _tile_slots(dest, tm)
    grid_spec = pltpu.PrefetchScalarGridSpec(
        num_scalar_prefetch=2,
        grid=(n // tm,),
        in_specs=[pl.BlockSpec((1, 1, 2 * tm), lambda i, ps, pe: (i, 0, 0), memory_space=pltpu.SMEM),
                  pl.BlockSpec(_tiled(tm), lambda i, ps, pe: (i, 0))],
        out_specs=pl.BlockSpec(memory_space=pl.ANY),
        scratch_shapes=[pltpu.VMEM(_tiled(SLOT_BLOCK), F32), pltpu.SemaphoreType.DMA(())],
    )
    return pl.pallas_call(
        _dispatch_kernel,
        grid_spec=grid_spec,
        out_shape=jax.ShapeDtypeStruct(_tiled(n_slots), F32),
        compiler_params=_params(("arbitrary",)),
        name="dispatch",
    )(pad_start, pad_end, dest, u)


def _expert_kernel(blk_e_ref, used_ref, x_ref, wg_ref, wu_ref, wd_ref, y_ref, wg_bf, wu_bf, wd_bf):
    j = pl.program_id(0)

    @pl.when(jnp.logical_or(j == 0, blk_e_ref[j] != blk_e_ref[jnp.maximum(j - 1, 0)]))
    def _():
        wg_bf[...] = wg_ref[0].astype(BF16)
        wu_bf[...] = wu_ref[0].astype(BF16)
        wd_bf[...] = wd_ref[0].astype(BF16)

    @pl.when(j < used_ref[0])
    def _():
        x = _load_row_tiles(x_ref).astype(BF16)
        a = _dot(x, wg_bf[...])
        b = _dot(x, wu_bf[...])
        hid = (a * (1.0 / (1.0 + jnp.exp(-a))) * b).astype(BF16)
        _store_row_tiles(y_ref, _dot(hid, wd_bf[...]))

    @pl.when(pl.program_id(0) >= used_ref[0])
    def _():
        y_ref[...] = jnp.zeros_like(y_ref)


def _experts(slots, blk_e, used, wg, wu, wd):
    n_slots = slots.shape[0] // ROW_SUB
    nb = n_slots // SLOT_BLOCK
    xmap = lambda i, be, us: (jnp.minimum(i, us[0] - 1), 0)
    wmap = lambda i, be, us: (be[i], 0, 0)
    grid_spec = pltpu.PrefetchScalarGridSpec(
        num_scalar_prefetch=2,
        grid=(nb,),
        in_specs=[pl.BlockSpec(_tiled(SLOT_BLOCK), xmap),
                  pl.BlockSpec((1, D_MODEL, D_EXPERT), wmap),
                  pl.BlockSpec((1, D_MODEL, D_EXPERT), wmap),
                  pl.BlockSpec((1, D_EXPERT, D_MODEL), wmap)],
        out_specs=pl.BlockSpec(_tiled(SLOT_BLOCK), lambda i, be, us: (i, 0)),
        scratch_shapes=[pltpu.VMEM((D_MODEL, D_EXPERT), BF16), pltpu.VMEM((D_MODEL, D_EXPERT), BF16),
                        pltpu.VMEM((D_EXPERT, D_MODEL), BF16)],
    )
    return pl.pallas_call(
        _expert_kernel,
        grid_spec=grid_spec,
        out_shape=jax.ShapeDtypeStruct(_tiled(n_slots), F32),
        compiler_params=_params(("arbitrary",)),
        name="experts",
    )(blk_e, used, slots, wg, wu, wd)


def _combine_kernel(dest_ref, dest_next_ref, h_ref, gate_ref, p_ref, wp_ref, wg_ref, gp_ref,
                    gf_ref, y_ref, out_ref, buf, sems):
    tm = h_ref.shape[0]
    i = pl.program_id(0)

    def gather(dref, which, unrolled):
        _for_each_choice(dref, tm, lambda r, kk, slot: _row_copy(y_ref, slot, buf.at[which, kk], r,
                                                                 sems.at[which]).start(priority=kk), unrolled)

    def wait_tile(which):
        for kk in range(2):
            pltpu.make_async_copy(_rows(y_ref, 0, tm), buf.at[which, kk], sems.at[which]).wait()

    @pl.when(i == 0)
    def _():
        gather(dest_ref, 0, False)

    cur = i % 2
    wait_tile(cur)
    gather(dest_next_ref, 1 - cur, True)

    gates = gate_ref[...]
    moe = gates[:, 0:1] * _load_row_tiles(buf.at[cur, 0]) + gates[:, 1:2] * _load_row_tiles(buf.at[cur, 1])
    h = h_ref[...] + moe
    u = _rms(h, gp_ref[...]).astype(BF16)
    gate = 1.0 / (1.0 + jnp.exp(-_dot(u, wg_ref[...])))
    h = h + _dot(p_ref[...].astype(BF16), wp_ref[...]) * gate
    out_ref[...] = _rms(h, gf_ref[...])

    @pl.when(i + 1 == pl.num_programs(0))
    def _():
        wait_tile(1 - cur)


def _combine(dest, h, gate_col, p, wp, wg, gp, gf, y_sorted):
    n = h.shape[0]
    tm = COMBINE_TILE
    nt = n // tm
    dest = _tile_slots(dest, tm)
    row = lambda w: pl.BlockSpec((tm, w), lambda i: (i, 0))
    full = lambda a: pl.BlockSpec(a.shape, lambda i: (0,) * a.ndim)
    return pl.pallas_call(
        _combine_kernel,
        grid=(nt,),
        in_specs=[pl.BlockSpec((1, 1, 2 * tm), lambda i: (i, 0, 0), memory_space=pltpu.SMEM),
                  pl.BlockSpec((1, 1, 2 * tm), lambda i: (jnp.minimum(i + 1, nt - 1), 0, 0), memory_space=pltpu.SMEM),
                  row(D_MODEL), row(LANES), row(PLE_DIM), full(wp), full(wg), full(gp), full(gf),
                  pl.BlockSpec(memory_space=pl.ANY)],
        out_specs=row(D_MODEL),
        out_shape=jax.ShapeDtypeStruct((n, D_MODEL), F32),
        scratch_shapes=[pltpu.VMEM((2, 2) + _tiled(tm), F32), pltpu.SemaphoreType.DMA((2,))],
        compiler_params=_params(("arbitrary",)),
        name="combine_ple",
    )(dest, dest, h, gate_col, p, wp, wg, gp, gf, y_sorted)


def _pad_lanes(v, width=LANES):
    return jnp.zeros((1, width), F32).at[0, :v.shape[0]].set(v.astype(F32))


def _encoder(x3, p3, prm):
    batch, seq, _ = x3.shape
    n = batch * seq
    assert seq % SSD_TILE == 0 and (seq // GRID_W) % NA_ROWS == 0 and n % TOKEN_TILE == 0
    x = x3.reshape(n, D_MODEL)
    p = p3.reshape(n, PLE_DIM)

    z, xbc, q, k, v, dt = _inproj(x, prm["norm_mix"], prm["w_z"], prm["w_xbc"], prm["w_q"], prm["w_k"],
                                  prm["w_v"], prm["w_dt"])

    y_bwd, act = _ssd_pass(xbc, dt, prm["dt_bias"][1], prm["a_log"][1], batch=batch, seq=seq, backward=True,
                           extra=(prm["conv_w"], prm["conv_b"]))
    y_ssd = _ssd_pass(act, dt, prm["dt_bias"][0], prm["a_log"][0], batch=batch, seq=seq, backward=False,
                      extra=(z, y_bwd, prm["d_exp"], prm["ssd_norm"]))
    y_att = _natten(q, k, v, prm["bias_tab"], prm["attn_norm"], batch=batch, seq=seq)

    h1, u2, route, gate_col, counts = _outproj(x, y_ssd, y_att, prm["w_out_a"], prm["w_out_b"], prm["norm_ffn"],
                                               prm["w_router"], prm["upper"])

    cnt = counts[:, 0].astype(jnp.int32)
    padded = (cnt + SLOT_BLOCK - 1) // SLOT_BLOCK * SLOT_BLOCK
    pad_end = jnp.cumsum(padded)
    pad_start = pad_end - padded
    n_blocks = -(-(2 * n + N_EXPERTS * (SLOT_BLOCK - 1)) // SLOT_BLOCK)
    n_slots = n_blocks * SLOT_BLOCK
    first_slot = jnp.arange(n_blocks, dtype=jnp.int32) * SLOT_BLOCK
    blk_e = jnp.minimum(jnp.sum((pad_end[None, :] <= first_slot[:, None]).astype(jnp.int32), axis=1), N_EXPERTS - 1)
    used = pad_end[-1:] // SLOT_BLOCK
    hot = route[0:2][None] == jnp.arange(N_EXPERTS, dtype=jnp.int32)[:, None, None]
    dest = route[2:4] + jnp.sum(jnp.where(hot, pad_start[:, None, None], 0), axis=0)

    slots = _dispatch(u2, dest, pad_start, pad_end, n_slots)
    y_sorted = _experts(slots, blk_e, used, prm["w_gate"], prm["w_up"], prm["w_down"])
    out = _combine(dest, h1, gate_col, p, prm["w_ple_proj"], prm["w_ple_gate"], prm["norm_ple"],
                   prm["norm_final"], y_sorted)
    return out.reshape(batch, seq, D_MODEL)


def _prepare(w_in, conv_w, conv_b, dt_bias, a_log, d_skip, ssd_norm, rel_bias, attn_norm, w_out, norm_mix,
             norm_ffn, router_group, router_expert, w_gate, w_up, w_down, norm_ple, w_ple_proj, w_ple_gate,
             norm_final):
    w_in = w_in[0]
    offs = np.cumsum([0, SSD_WIDTH, CONV_DIM, 2 * N_HEADS, D_MODEL, D_MODEL, D_MODEL])
    piece = lambda i: w_in[:, offs[i]:offs[i + 1]]
    w_dt = jnp.zeros((D_MODEL, 2 * LANES), F32)
    w_dt = w_dt.at[:, :N_HEADS].set(piece(2)[:, :N_HEADS]).at[:, LANES:LANES + N_HEADS].set(piece(2)[:, N_HEADS:])
    row = lambda v: v.astype(F32).reshape(1, -1)
    w_router = jnp.zeros((ROUTER_ROWS, D_MODEL), F32)
    w_router = w_router.at[:N_GROUPS].set(router_group[0].T)
    w_router = w_router.at[8:8 + N_EXPERTS].set(router_expert[0].transpose(0, 2, 1).reshape(N_EXPERTS, D_MODEL))
    tm = TOKEN_TILE
    upper = jnp.asarray(np.triu(np.ones((tm, tm), np.float32), k=1), BF16)
    return {
        "norm_mix": row(norm_mix[0]),
        "w_z": piece(0).astype(BF16), "w_xbc": piece(1).astype(BF16), "w_dt": w_dt.astype(BF16),
        "w_q": piece(3).astype(BF16), "w_k": piece(4).astype(BF16), "w_v": piece(5).astype(BF16),
        "conv_w": jnp.zeros((8, CONV_DIM), F32).at[:D_CONV].set(conv_w[0]),
        "conv_b": row(conv_b[0]),
        "dt_bias": [_pad_lanes(dt_bias[0, d]) for d in range(2)],
        "a_log": [jnp.full((1, LANES), -jnp.inf, F32).at[0, :N_HEADS].set(a_log[0, d]) for d in range(2)],
        "d_exp": jnp.repeat(d_skip[0].astype(F32), HEAD_DIM).reshape(1, SSD_WIDTH),
        "ssd_norm": row(ssd_norm[0]),
        "bias_tab": _bias_table(rel_bias[0]),
        "attn_norm": row(attn_norm[0]),
        "w_out_a": w_out[0, :SSD_WIDTH].astype(BF16), "w_out_b": w_out[0, SSD_WIDTH:].astype(BF16),
        "norm_ffn": row(norm_ffn[0]),
        "w_router": w_router,
        "upper": upper,
        "w_gate": w_gate[0], "w_up": w_up[0], "w_down": w_down[0],
        "norm_ple": row(norm_ple[0]),
        "w_ple_proj": w_ple_proj[0].astype(BF16), "w_ple_gate": w_ple_gate[0].astype(BF16),
        "norm_final": row(norm_final),
    }


def kernel(x_prompt, x_sample, p_prompt, p_sample, w_in, conv_w, conv_b, dt_bias, a_log, d_skip, ssd_norm,
           rel_bias, attn_norm, w_out, norm_mix, norm_ffn, router_group, router_expert, w_gate, w_up, w_down,
           norm_ple, w_ple_proj, w_ple_gate, norm_final):
    prm = _prepare(w_in, conv_w, conv_b, dt_bias, a_log, d_skip, ssd_norm, rel_bias, attn_norm, w_out, norm_mix,
                   norm_ffn, router_group, router_expert, w_gate, w_up, w_down, norm_ple, w_ple_proj, w_ple_gate,
                   norm_final)
    y_prompt = _encoder(x_prompt, p_prompt[0], prm)
    y_sample = _encoder(x_sample, p_sample[0], prm)
    return (y_prompt, y_sample)
```

```python
import functools
import math

import numpy as np
import jax
import jax.numpy as jnp
from jax import lax
from jax.experimental import pallas as pl
from jax.experimental.pallas import tpu as pltpu

F32 = jnp.float32
BF16 = jnp.bfloat16

D_MODEL = 1024
SSD_WIDTH = 1024
HEAD_DIM = 64
N_HEADS = 16
N_BC_GROUPS = 2
D_STATE = 128
D_CONV = 5
CHUNK = 128
CONV_DIM = SSD_WIDTH + 2 * N_BC_GROUPS * D_STATE
HEADS_PER_GROUP = N_HEADS // N_BC_GROUPS
GROUP_WIDTH = SSD_WIDTH // N_BC_GROUPS
GRID_W = 64
WIN_H = 8
WIN_W = 16
N_GROUPS = 4
EXPERTS_PER_GROUP = 8
N_EXPERTS = N_GROUPS * EXPERTS_PER_GROUP
D_EXPERT = 512
PLE_DIM = 256
EPS = 1e-6
LOG2E = math.log2(math.e)

LANES = 128
HALO_ROWS = 16
TOKEN_TILE = 512
SSD_TILE = 512
CONV_COLS = 512
SSD_TRIP_CHUNKS = 4
NA_ROWS = 8
NA_TRIP_ROWS = 8
SLOT_BLOCK = 512
DISPATCH_TILE = 1024
COMBINE_TILE = 512
ROW_GROUP = 8
ROW_SUB = 8
ROUTER_ROWS = 48
VMEM_LIMIT = 56 * 1024 * 1024


def _rms(x, g):
    ms = jnp.mean(x * x, axis=-1, keepdims=True)
    return x * lax.rsqrt(ms + EPS) * g


def _split3(x):
    hi = x.astype(BF16)
    r1 = x - hi.astype(F32)
    mid = r1.astype(BF16)
    lo = (r1 - mid.astype(F32)).astype(BF16)
    return hi, mid, lo


def _dot(a, b):
    return jnp.dot(a, b, preferred_element_type=F32)


def _dot_nt(a, b):
    return lax.dot_general(a, b, (((1,), (1,)), ((), ())), preferred_element_type=F32)


def _tiled(m):
    return (m * ROW_SUB, LANES)


def _dot_tn(a, b):
    return lax.dot_general(a, b, (((0,), (0,)), ((), ())), preferred_element_type=F32)


def _store_row_tiles(ref, x):
    m = x.shape[0]
    for s in range(ROW_SUB):
        ref[pl.ds(s, m, stride=ROW_SUB), :] = x[:, s * LANES:(s + 1) * LANES]


def _load_row_tiles(ref):
    m = ref.shape[0] // ROW_SUB
    return jnp.concatenate([ref[pl.ds(s, m, stride=ROW_SUB), :] for s in range(ROW_SUB)], axis=1)


def _params(sem):
    return pltpu.CompilerParams(dimension_semantics=sem, vmem_limit_bytes=VMEM_LIMIT)


def _inproj_kernel(x_ref, g_ref, wz_ref, wxbc_ref, wq_ref, wk_ref, wv_ref, wdt_ref,
                   z_ref, xbc_ref, q_ref, k_ref, v_ref, dt_ref):
    u = _rms(x_ref[...], g_ref[...]).astype(BF16)
    z_ref[...] = _dot(u, wz_ref[...]).astype(BF16)
    xbc_ref[...] = _dot(u, wxbc_ref[...]).astype(BF16)
    q_ref[...] = (_dot(u, wq_ref[...]) * (HEAD_DIM ** -0.5 * LOG2E)).astype(BF16)
    k_ref[...] = _dot(u, wk_ref[...]).astype(BF16)
    v_ref[...] = _dot(u, wv_ref[...]).astype(BF16)
    dt_ref[...] = _dot(u, wdt_ref[...])


def _inproj(x, g, wz, wxbc, wq, wk, wv, wdt):
    n = x.shape[0]
    tm = TOKEN_TILE
    row = lambda w: pl.BlockSpec((tm, w), lambda i: (i, 0))
    full = lambda a: pl.BlockSpec(a.shape, lambda i: (0,) * a.ndim)
    outs = [(D_MODEL, BF16), (CONV_DIM, BF16), (D_MODEL, BF16), (D_MODEL, BF16), (D_MODEL, BF16),
            (2 * LANES, F32)]
    return pl.pallas_call(
        _inproj_kernel,
        grid=(n // tm,),
        in_specs=[row(D_MODEL), full(g), full(wz), full(wxbc), full(wq), full(wk), full(wv), full(wdt)],
        out_specs=[row(w) for w, _ in outs],
        out_shape=[jax.ShapeDtypeStruct((n, w), d) for w, d in outs],
        compiler_params=_params(("parallel",)),
        name="inproj",
    )(x, g, wz, wxbc, wq, wk, wv, wdt)


def _ssd_kernel(*refs, backward, n_blocks):
    if backward:
        (xbc_ref, prev_ref, next_ref, dt_ref, cw_ref, cb_ref, dtb_ref, alog_ref,
         out_ref, act_ref, xp_ref, st_ref) = refs
    else:
        (act_ref, dt_ref, dtb_ref, alog_ref, z_ref, ybwd_ref, dexp_ref, gn_ref, out_ref, st_ref) = refs
    tb = act_ref.shape[0]
    t = CHUNK
    n_chunks = tb // t
    j = pl.program_id(1)

    @pl.when(j == 0)
    def _():
        st_ref[...] = jnp.zeros_like(st_ref)

    if backward:
        blk = n_blocks - 1 - j
        hr = HALO_ROWS
        xp_ref[0:hr, :] = jnp.where(blk == 0, jnp.zeros_like(prev_ref), prev_ref[...])
        xp_ref[hr:hr + tb, :] = xbc_ref[...]
        xp_ref[hr + tb:, :] = jnp.where(blk == n_blocks - 1, jnp.zeros_like(next_ref), next_ref[...])
        taps = [kk for kk in range(D_CONV) if kk != D_CONV // 2]
        win = t + 2 * hr
        srow = lax.broadcasted_iota(jnp.int32, (len(taps) * t, win), 0)
        scol = lax.broadcasted_iota(jnp.int32, (len(taps) * t, win), 1)
        tap_id = srow // t
        tap_off = tap_id + jnp.where(tap_id >= D_CONV // 2, 1, 0) + (hr - D_CONV // 2)
        shift_sel = jnp.where(scol == (srow % t) + tap_off, 1.0, 0.0).astype(BF16)
        for c in range(n_chunks):
            for c0 in range(0, CONV_DIM, CONV_COLS):
                cols = slice(c0, c0 + CONV_COLS)
                window = xp_ref[c * t:c * t + win, cols]
                shifted = _dot(shift_sel, window)
                acc = cb_ref[:, cols] + cw_ref[D_CONV // 2:D_CONV // 2 + 1, cols] * window[hr:hr + t].astype(F32)
                for ti, kk in enumerate(taps):
                    acc = acc + cw_ref[kk:kk + 1, cols] * shifted[ti * t:(ti + 1) * t]
                act_ref[c * t:(c + 1) * t, cols] = (acc * (1.0 / (1.0 + jnp.exp(-acc)))).astype(act_ref.dtype)

    a_row = -jnp.exp(alog_ref[...]) * LOG2E
    dtb = dtb_ref[...]
    ri = lax.broadcasted_iota(jnp.int32, (t, t), 0)
    ci = lax.broadcasted_iota(jnp.int32, (t, t), 1)
    if backward:
        keep = ci >= ri
    else:
        keep = ri >= ci
    tri = jnp.where(keep, 1.0, 0.0).astype(BF16)
    er = lax.broadcasted_iota(jnp.int32, (2 * LANES, SSD_WIDTH), 0)
    ec = lax.broadcasted_iota(jnp.int32, (2 * LANES, SSD_WIDTH), 1)
    expand = jnp.where((er % LANES) == (ec // HEAD_DIM), 1.0, 0.0).astype(BF16)
    lane = lax.broadcasted_iota(jnp.int32, (t, LANES), 1)
    low_half = lane < HEAD_DIM

    def chunk(c, st):
        r0 = pl.multiple_of(c * t, t)
        act = act_ref[pl.ds(r0, t), :]
        xs_bf = act[:, :SSD_WIDTH]
        xs = xs_bf.astype(F32)
        groups = []
        for g in range(N_BC_GROUPS):
            b_g = act[:, SSD_WIDTH + g * D_STATE:SSD_WIDTH + (g + 1) * D_STATE]
            c_g = act[:, SSD_WIDTH + (N_BC_GROUPS + g) * D_STATE:SSD_WIDTH + (N_BC_GROUPS + g + 1) * D_STATE]
            groups.append((b_g, c_g, slice(g * GROUP_WIDTH, (g + 1) * GROUP_WIDTH)))

        st_bf = st.astype(BF16)
        cbs = [_dot_nt(c_g, b_g) for b_g, c_g, _ in groups]
        y_offs = [_dot(c_g, st_bf[:, hs]) for _, c_g, hs in groups]

        dt_raw = dt_ref[pl.ds(r0, t), :] + dtb
        dt = jnp.maximum(dt_raw, 0.0) + jnp.log(1.0 + jnp.exp(-jnp.abs(dt_raw)))
        hi, mid, lo = _split3(dt * a_row)
        cs = _dot(tri, hi) + _dot(tri, mid) + _dot(tri, lo)
        cs_row = cs.T
        dt_row = dt.T
        edge = cs[0:1, :] if backward else cs[t - 1:t, :]
        w_in = dt * jnp.exp2(edge - cs)
        ecs = jnp.exp2(cs)
        both = jnp.concatenate([w_in, ecs], axis=0)
        b_hi = both.astype(BF16)
        b_lo = (both - b_hi.astype(F32)).astype(BF16)
        both_x = _dot(jnp.concatenate([b_hi, b_lo], axis=1), expand)
        w_x = both_x[:t]
        ecs_x = both_x[t:]
        cd_x = ecs_x[0:1, :] if backward else ecs_x[t - 1:t, :]

        y_parts = []
        for g, (_, _, hs) in enumerate(groups):
            pair_parts = []
            for pp in range(HEADS_PER_GROUP // 2):
                h0 = g * HEADS_PER_GROUP + 2 * pp
                xs_pair = xs_bf[:, h0 * HEAD_DIM:(h0 + 2) * HEAD_DIM]
                ys = []
                for h in (h0, h0 + 1):
                    diff = cs[:, h:h + 1] - cs_row[h:h + 1, :]
                    seg = jnp.exp2(jnp.where(keep, diff, -jnp.inf))
                    m = (cbs[g] * seg * dt_row[h:h + 1, :]).astype(BF16)
                    ys.append(_dot(m, xs_pair))
                pair_parts.append(jnp.where(low_half, ys[0], ys[1]))
            y_parts.append(jnp.concatenate(pair_parts, axis=1) + y_offs[g] * ecs_x[:, hs])
        y = jnp.concatenate(y_parts, axis=1)

        xw = (xs * w_x).astype(BF16)
        news = [_dot_tn(b_g, xw[:, hs]) for b_g, _, hs in groups]
        st_new = st * cd_x + jnp.concatenate(news, axis=1)

        if backward:
            out_ref[pl.ds(r0, t), :] = y.astype(out_ref.dtype)
        else:
            y = y + ybwd_ref[pl.ds(r0, t), :].astype(F32) + dexp_ref[...] * xs
            zz = z_ref[pl.ds(r0, t), :].astype(F32)
            y = y * (zz * (1.0 / (1.0 + jnp.exp(-zz))))
            out_ref[pl.ds(r0, t), :] = _rms(y, gn_ref[...]).astype(out_ref.dtype)
        return st_new

    def trip(i, carry):
        st = st_ref[...]
        for u in range(SSD_TRIP_CHUNKS):
            ci = i * SSD_TRIP_CHUNKS + u
            st = chunk((n_chunks - 1 - ci) if backward else ci, st)
        st_ref[...] = st
        return carry

    lax.fori_loop(0, n_chunks // SSD_TRIP_CHUNKS, trip, 0)


def _ssd_pass(xin, dt, dtb, alog, *, batch, seq, backward, extra=()):
    n = xin.shape[0]
    tb = SSD_TILE
    nb = seq // tb
    hb = tb // HALO_ROWS
    n_halo = n // HALO_ROWS
    d = 1 if backward else 0

    def blk_of(b, j):
        return b * nb + ((nb - 1 - j) if backward else j)

    main = lambda w: pl.BlockSpec((tb, w), lambda b, j: (blk_of(b, j), 0))
    full = lambda a: pl.BlockSpec(a.shape, lambda b, j: (0,) * a.ndim)
    dt_spec = pl.BlockSpec((tb, LANES), lambda b, j: (blk_of(b, j), d))
    state = pltpu.VMEM((D_STATE, SSD_WIDTH), F32)
    y_shape = jax.ShapeDtypeStruct((n, SSD_WIDTH), BF16)
    if backward:
        cw, cb = extra
        in_specs = [
            main(CONV_DIM),
            pl.BlockSpec((HALO_ROWS, CONV_DIM), lambda b, j: (jnp.maximum(blk_of(b, j) * hb - 1, 0), 0)),
            pl.BlockSpec((HALO_ROWS, CONV_DIM), lambda b, j: (jnp.minimum((blk_of(b, j) + 1) * hb, n_halo - 1), 0)),
            dt_spec, full(cw), full(cb), full(dtb), full(alog),
        ]
        args = [xin, xin, xin, dt, cw, cb, dtb, alog]
        out_specs = [main(SSD_WIDTH), main(CONV_DIM)]
        out_shape = [y_shape, jax.ShapeDtypeStruct((n, CONV_DIM), BF16)]
        scratch = [pltpu.VMEM((tb + 2 * HALO_ROWS, CONV_DIM), BF16), state]
    else:
        z, ybwd, dexp, gn = extra
        in_specs = [main(CONV_DIM), dt_spec, full(dtb), full(alog), main(SSD_WIDTH), main(SSD_WIDTH), full(dexp),
                    full(gn)]
        args = [xin, dt, dtb, alog, z, ybwd, dexp, gn]
        out_specs = main(SSD_WIDTH)
        out_shape = y_shape
        scratch = [state]
    return pl.pallas_call(
        functools.partial(_ssd_kernel, backward=backward, n_blocks=nb),
        grid=(batch, nb),
        in_specs=in_specs,
        out_specs=out_specs,
        out_shape=out_shape,
        scratch_shapes=scratch,
        compiler_params=_params(("parallel", "arbitrary")),
        name="ssd_bwd" if backward else "ssd_fwd",
    )(*args)


def _natten_kernel(q_ref, kp_ref, kc_ref, kn_ref, vp_ref, vc_ref, vn_ref, bias_ref, gn_ref,
                   out_ref, kbuf, vbuf, obuf, *, rows):
    half = NA_ROWS // 2 * GRID_W
    full = NA_ROWS * GRID_W
    jb = pl.program_id(1)
    kbuf[0:half, :] = kp_ref[...]
    kbuf[half:half + full, :] = kc_ref[...]
    kbuf[half + full:, :] = kn_ref[...]
    vbuf[0:half, :] = vp_ref[...]
    vbuf[half:half + full, :] = vc_ref[...]
    vbuf[half + full:, :] = vn_ref[...]
    lane = lax.broadcasted_iota(jnp.int32, (GRID_W, LANES), 1)
    low_half = lane < HEAD_DIM
    n_keys = WIN_H * GRID_W
    ones_keys = jnp.ones((n_keys, LANES), BF16)

    n_pairs = N_HEADS // 2
    lanes_of = lambda pp: slice(pp * LANES, (pp + 1) * LANES)

    def trip_body(it, carry):
        geo = []
        for rr in range(NA_TRIP_ROWS):
            i = it * NA_TRIP_ROWS + rr
            r = jb * NA_ROWS + i
            s = jnp.clip(r - WIN_H // 2, 0, rows - WIN_H)
            local = s - (jb * NA_ROWS - NA_ROWS // 2)
            geo.append((pl.multiple_of(local * GRID_W, GRID_W), pl.multiple_of(i * GRID_W, GRID_W),
                        s - r + (WIN_H - 1)))
        scores = []
        for k0, q0, d0 in geo:
            for pp in range(n_pairs):
                q_pair = q_ref[pl.ds(q0, GRID_W), lanes_of(pp)]
                zero = jnp.zeros_like(q_pair)
                q2 = jnp.concatenate([jnp.where(low_half, q_pair, zero), jnp.where(low_half, zero, q_pair)], axis=0)
                sc = _dot_nt(q2, kbuf[pl.ds(k0, n_keys), lanes_of(pp)])
                bias = jnp.concatenate(
                    [jnp.concatenate([bias_ref[d0 + 2 * ii, 2 * pp + hh] for ii in range(WIN_H // 2)], axis=1)
                     for hh in range(2)], axis=0)
                scores.append(sc + bias)
        probs = []
        for sc in scores:
            m = jnp.max(sc, axis=-1, keepdims=True)
            probs.append(jnp.exp2(sc - m).astype(BF16))
        for rr, (k0, q0, d0) in enumerate(geo):
            for pp in range(n_pairs):
                v_ext = jnp.concatenate([vbuf[pl.ds(k0, n_keys), lanes_of(pp)], ones_keys], axis=1)
                o_ext = _dot(probs[rr * n_pairs + pp], v_ext)
                o2 = o_ext[:, :LANES] * (1.0 / o_ext[:, LANES:])
                obuf[rr * GRID_W:(rr + 1) * GRID_W, lanes_of(pp)] = jnp.where(low_half, o2[:GRID_W], o2[GRID_W:])
        for rr, (k0, q0, d0) in enumerate(geo):
            o = obuf[rr * GRID_W:(rr + 1) * GRID_W, :]
            out_ref[pl.ds(q0, GRID_W), :] = _rms(o, gn_ref[...]).astype(out_ref.dtype)
        return carry

    lax.fori_loop(0, NA_ROWS // NA_TRIP_ROWS, trip_body, 0)


def _natten(q, k, v, bias_tab, gn, *, batch, seq):
    n = q.shape[0]
    rows = seq // GRID_W
    nrb = rows // NA_ROWS
    full = NA_ROWS * GRID_W
    half = full // 2
    n_half = n // half
    cur = pl.BlockSpec((full, D_MODEL), lambda b, j: (b * nrb + j, 0))
    prev = pl.BlockSpec((half, D_MODEL), lambda b, j: (jnp.maximum((b * nrb + j) * 2 - 1, 0), 0))
    nxt = pl.BlockSpec((half, D_MODEL), lambda b, j: (jnp.minimum((b * nrb + j) * 2 + 2, n_half - 1), 0))
    const = lambda a: pl.BlockSpec(a.shape, lambda b, j: (0,) * a.ndim)
    return pl.pallas_call(
        functools.partial(_natten_kernel, rows=rows),
        grid=(batch, nrb),
        in_specs=[cur, prev, cur, nxt, prev, cur, nxt, const(bias_tab), const(gn)],
        out_specs=cur,
        out_shape=jax.ShapeDtypeStruct((n, D_MODEL), BF16),
        scratch_shapes=[pltpu.VMEM((2 * full, D_MODEL), BF16), pltpu.VMEM((2 * full, D_MODEL), BF16),
                        pltpu.VMEM((NA_TRIP_ROWS * GRID_W, D_MODEL), F32)],
        compiler_params=_params(("parallel", "parallel")),
        name="natten",
    )(q, k, k, k, v, v, v, bias_tab, gn)


def _bias_table(rel_bias):
    qc = np.arange(GRID_W)[:, None]
    kc = np.arange(GRID_W)[None, :]
    s_c = np.clip(qc - WIN_W // 2, 0, GRID_W - WIN_W)
    inside = (kc >= s_c) & (kc < s_c + WIN_W)
    dcol = np.clip(kc - qc, -(WIN_W - 1), WIN_W - 1) + WIN_W - 1
    tab = rel_bias.astype(F32)[:, :, dcol]
    tab = jnp.where(inside[None, None], tab * LOG2E, -jnp.inf)
    pairs = jnp.concatenate([tab[:, :-1], tab[:, 1:]], axis=-1)
    return pairs.transpose(1, 0, 2, 3)


def _outproj_kernel(x_ref, ys_ref, ya_ref, wa_ref, wb_ref, gn_ref, wr_ref, upper_ref,
                    h_ref, u_ref, idx_ref, gate_ref, cnt_ref, carry_ref):
    tm = x_ref.shape[0]

    @pl.when(pl.program_id(0) == 0)
    def _():
        carry_ref[...] = jnp.zeros_like(carry_ref)

    h = x_ref[...] + _dot(ys_ref[...], wa_ref[...]) + _dot(ya_ref[...], wb_ref[...])
    h_ref[...] = h
    u = _rms(h, gn_ref[...])
    _store_row_tiles(u_ref, u)

    u_hi = u.astype(BF16)
    u_lo = (u - u_hi.astype(F32)).astype(BF16)
    wr = wr_ref[...]
    w_hi = wr.astype(BF16)
    w_lo = (wr - w_hi.astype(F32)).astype(BF16)
    both = _dot_nt(jnp.concatenate([w_hi, w_lo], axis=0), u_hi)
    logit = both[:ROUTER_ROWS] + both[ROUTER_ROWS:] + _dot_nt(w_hi, u_lo)

    lg = logit[0:N_GROUPS]
    gi = lax.broadcasted_iota(jnp.int32, (N_GROUPS, tm), 0)
    g_max = jnp.max(lg, axis=0, keepdims=True)
    g_idx = jnp.min(jnp.where(lg == g_max, gi, N_GROUPS), axis=0, keepdims=True)
    g_val = 1.0 / jnp.sum(jnp.exp(lg - g_max), axis=0, keepdims=True)

    le = logit[8:8 + EXPERTS_PER_GROUP]
    for g in range(1, N_GROUPS):
        le = jnp.where(g_idx == g, logit[8 + g * EXPERTS_PER_GROUP:8 + (g + 1) * EXPERTS_PER_GROUP], le)
    ei = lax.broadcasted_iota(jnp.int32, (EXPERTS_PER_GROUP, tm), 0)
    m1 = jnp.max(le, axis=0, keepdims=True)
    i1 = jnp.min(jnp.where(le == m1, ei, EXPERTS_PER_GROUP), axis=0, keepdims=True)
    le2 = jnp.where(ei == i1, -jnp.inf, le)
    m2 = jnp.max(le2, axis=0, keepdims=True)
    i2 = jnp.min(jnp.where(le2 == m2, ei, EXPERTS_PER_GROUP), axis=0, keepdims=True)
    e21 = jnp.exp(m2 - m1)
    den = 1.0 / (1.0 + e21)
    gate1 = g_val * den
    gate2 = g_val * e21 * den
    e1 = g_idx * EXPERTS_PER_GROUP + i1
    e2 = g_idx * EXPERTS_PER_GROUP + i2

    xi = lax.broadcasted_iota(jnp.int32, (N_EXPERTS, tm), 0)
    hot1 = xi == e1
    hot2 = xi == e2
    hot = jnp.where(jnp.logical_or(hot1, hot2), 1.0, 0.0)
    before = _dot(hot.astype(BF16), upper_ref[...])
    carry = carry_ref[...]
    before = before + jnp.tile(carry, (1, tm // LANES))
    rank1 = jnp.sum(jnp.where(hot1, before, 0.0), axis=0, keepdims=True)
    rank2 = jnp.sum(jnp.where(hot2, before, 0.0), axis=0, keepdims=True)
    carry = carry + jnp.sum(hot, axis=1, keepdims=True)
    carry_ref[...] = carry
    cnt_ref[...] = carry

    zi = jnp.zeros((4, tm), jnp.int32)
    idx_ref[...] = jnp.concatenate([e1, e2, rank1.astype(jnp.int32), rank2.astype(jnp.int32), zi], axis=0)
    gates = jnp.concatenate([gate1, gate2, jnp.zeros((LANES - 2, tm), F32)], axis=0)
    gate_ref[...] = gates.T


def _outproj(x, y_ssd, y_att, wa, wb, gn, wr, upper):
    n = x.shape[0]
    tm = TOKEN_TILE
    row = lambda w: pl.BlockSpec((tm, w), lambda i: (i, 0))
    full = lambda a: pl.BlockSpec(a.shape, lambda i: (0,) * a.ndim)
    return pl.pallas_call(
        _outproj_kernel,
        grid=(n // tm,),
        in_specs=[row(D_MODEL), row(D_MODEL), row(D_MODEL), full(wa), full(wb), full(gn), full(wr), full(upper)],
        out_specs=[row(D_MODEL), pl.BlockSpec(_tiled(tm), lambda i: (i, 0)),
                   pl.BlockSpec((8, tm), lambda i: (0, i)), row(LANES),
                   pl.BlockSpec((N_EXPERTS, LANES), lambda i: (0, 0))],
        out_shape=[jax.ShapeDtypeStruct((n, D_MODEL), F32), jax.ShapeDtypeStruct(_tiled(n), F32),
                   jax.ShapeDtypeStruct((8, n), jnp.int32), jax.ShapeDtypeStruct((n, LANES), F32),
                   jax.ShapeDtypeStruct((N_EXPERTS, LANES), F32)],
        scratch_shapes=[pltpu.VMEM((N_EXPERTS, LANES), F32)],
        compiler_params=_params(("arbitrary",)),
        name="outproj_router",
    )(x, y_ssd, y_att, wa, wb, gn, wr, upper)


def _rows(ref, first, count):
    return ref.at[pl.ds(pl.multiple_of(first * ROW_SUB, ROW_SUB), count * ROW_SUB)]


def _row_copy(src_ref, src_row, dst_ref, dst_row, sem):
    return pltpu.make_async_copy(_rows(src_ref, src_row, 1), _rows(dst_ref, dst_row, 1), sem)


def _for_each_choice(dest_ref, n_rows, fn, unrolled=False):
    if unrolled:
        for r in range(n_rows):
            for kk in range(2):
                fn(r, kk, dest_ref[0, 0, kk * n_rows + r])
        return

    def group(g, carry):
        base = pl.multiple_of(g * ROW_GROUP, ROW_GROUP)
        for jj in range(ROW_GROUP):
            for kk in range(2):
                fn(base + jj, kk, dest_ref[0, 0, kk * n_rows + base + jj])
        return carry

    lax.fori_loop(0, n_rows // ROW_GROUP, group, 0)


def _dispatch_kernel(pstart_ref, pend_ref, dest_ref, u_ref, slots_ref, zero_ref, sem):
    tm = u_ref.shape[0] // ROW_SUB

    @pl.when(pl.program_id(0) == 0)
    def _():
        zero_ref[...] = jnp.zeros_like(zero_ref)

        def fill(e):
            return pltpu.make_async_copy(zero_ref, _rows(slots_ref, pend_ref[e] - SLOT_BLOCK, SLOT_BLOCK), sem)

        for e in range(N_EXPERTS):
            @pl.when(pend_ref[e] > pstart_ref[e])
            def _():
                fill(e).start()
        for e in range(N_EXPERTS):
            @pl.when(pend_ref[e] > pstart_ref[e])
            def _():
                fill(e).wait()

        def tail(j):
            return pltpu.make_async_copy(zero_ref, _rows(slots_ref, j * SLOT_BLOCK, SLOT_BLOCK), sem)

        used = lax.div(pend_ref[N_EXPERTS - 1], SLOT_BLOCK)
        n_blocks = slots_ref.shape[0] // (SLOT_BLOCK * ROW_SUB)

        def tail_start(j, carry):
            tail(j).start()
            return carry

        def tail_wait(j, carry):
            tail(j).wait()
            return carry

        lax.fori_loop(used, n_blocks, tail_start, 0)
        lax.fori_loop(used, n_blocks, tail_wait, 0)

    _for_each_choice(dest_ref, tm, lambda r, kk, slot: _row_copy(u_ref, r, slots_ref, slot, sem).start(priority=kk))
    for _ in range(2):
        pltpu.make_async_copy(u_ref, _rows(slots_ref, 0, tm), sem).wait()


def _tile_slots(dest, tm):
    nt = dest.shape[1] // tm
    return dest.reshape(2, nt, tm).transpose(1, 0, 2).reshape(nt, 1, 2 * tm)


def _dispatch(u, dest, pad_start, pad_end, n_slots):
    n = u.shape[0] // ROW_SUB
    tm = DISPATCH_TILE
    dest = _tile_slots(dest, tm)
    grid_spec = pltpu.PrefetchScalarGridSpec(
        num_scalar_prefetch=2,
        grid=(n // tm,),
        in_specs=[pl.BlockSpec((1, 1, 2 * tm), lambda i, ps, pe: (i, 0, 0), memory_space=pltpu.SMEM),
                  pl.BlockSpec(_tiled(tm), lambda i, ps, pe: (i, 0))],
        out_specs=pl.BlockSpec(memory_space=pl.ANY),
        scratch_shapes=[pltpu.VMEM(_tiled(SLOT_BLOCK), F32), pltpu.SemaphoreType.DMA(())],
    )
    return pl.pallas_call(
        _dispatch_kernel,
        grid_spec=grid_spec,
        out_shape=jax.ShapeDtypeStruct(_tiled(n_slots), F32),
        compiler_params=_params(("arbitrary",)),
        name="dispatch",
    )(pad_start, pad_end, dest, u)


def _expert_kernel(blk_e_ref, used_ref, x_ref, wg_ref, wu_ref, wd_ref, y_ref, wg_bf, wu_bf, wd_bf):
    j = pl.program_id(0)

    @pl.when(jnp.logical_or(j == 0, blk_e_ref[j] != blk_e_ref[jnp.maximum(j - 1, 0)]))
    def _():
        wg_bf[...] = wg_ref[0].astype(BF16)
        wu_bf[...] = wu_ref[0].astype(BF16)
        wd_bf[...] = wd_ref[0].astype(BF16)

    @pl.when(j < used_ref[0])
    def _():
        x = _load_row_tiles(x_ref).astype(BF16)
        a = _dot(x, wg_bf[...])
        b = _dot(x, wu_bf[...])
        hid = (a * (1.0 / (1.0 + jnp.exp(-a))) * b).astype(BF16)
        _store_row_tiles(y_ref, _dot(hid, wd_bf[...]))

    @pl.when(pl.program_id(0) >= used_ref[0])
    def _():
        y_ref[...] = jnp.zeros_like(y_ref)


def _experts(slots, blk_e, used, wg, wu, wd):
    n_slots = slots.shape[0] // ROW_SUB
    nb = n_slots // SLOT_BLOCK
    xmap = lambda i, be, us: (jnp.minimum(i, us[0] - 1), 0)
    wmap = lambda i, be, us: (be[i], 0, 0)
    grid_spec = pltpu.PrefetchScalarGridSpec(
        num_scalar_prefetch=2,
        grid=(nb,),
        in_specs=[pl.BlockSpec(_tiled(SLOT_BLOCK), xmap),
                  pl.BlockSpec((1, D_MODEL, D_EXPERT), wmap),
                  pl.BlockSpec((1, D_MODEL, D_EXPERT), wmap),
                  pl.BlockSpec((1, D_EXPERT, D_MODEL), wmap)],
        out_specs=pl.BlockSpec(_tiled(SLOT_BLOCK), lambda i, be, us: (i, 0)),
        scratch_shapes=[pltpu.VMEM((D_MODEL, D_EXPERT), BF16), pltpu.VMEM((D_MODEL, D_EXPERT), BF16),
                        pltpu.VMEM((D_EXPERT, D_MODEL), BF16)],
    )
    return pl.pallas_call(
        _expert_kernel,
        grid_spec=grid_spec,
        out_shape=jax.ShapeDtypeStruct(_tiled(n_slots), F32),
        compiler_params=_params(("arbitrary",)),
        name="experts",
    )(blk_e, used, slots, wg, wu, wd)


def _combine_kernel(dest_ref, dest_next_ref, h_ref, gate_ref, p_ref, wp_ref, wg_ref, gp_ref,
                    gf_ref, y_ref, out_ref, buf, sems):
    tm = h_ref.shape[0]
    i = pl.program_id(0)

    def gather(dref, which, unrolled):
        _for_each_choice(dref, tm, lambda r, kk, slot: _row_copy(y_ref, slot, buf.at[which, kk], r,
                                                                 sems.at[which]).start(priority=kk), unrolled)

    def wait_tile(which):
        for kk in range(2):
            pltpu.make_async_copy(_rows(y_ref, 0, tm), buf.at[which, kk], sems.at[which]).wait()

    @pl.when(i == 0)
    def _():
        gather(dest_ref, 0, False)

    cur = i % 2
    wait_tile(cur)
    gather(dest_next_ref, 1 - cur, True)

    gates = gate_ref[...]
    moe = gates[:, 0:1] * _load_row_tiles(buf.at[cur, 0]) + gates[:, 1:2] * _load_row_tiles(buf.at[cur, 1])
    h = h_ref[...] + moe
    u = _rms(h, gp_ref[...]).astype(BF16)
    gate = 1.0 / (1.0 + jnp.exp(-_dot(u, wg_ref[...])))
    h = h + _dot(p_ref[...].astype(BF16), wp_ref[...]) * gate
    out_ref[...] = _rms(h, gf_ref[...])

    @pl.when(i + 1 == pl.num_programs(0))
    def _():
        wait_tile(1 - cur)


def _combine(dest, h, gate_col, p, wp, wg, gp, gf, y_sorted):
    n = h.shape[0]
    tm = COMBINE_TILE
    nt = n // tm
    dest = _tile_slots(dest, tm)
    row = lambda w: pl.BlockSpec((tm, w), lambda i: (i, 0))
    full = lambda a: pl.BlockSpec(a.shape, lambda i: (0,) * a.ndim)
    return pl.pallas_call(
        _combine_kernel,
        grid=(nt,),
        in_specs=[pl.BlockSpec((1, 1, 2 * tm), lambda i: (i, 0, 0), memory_space=pltpu.SMEM),
                  pl.BlockSpec((1, 1, 2 * tm), lambda i: (jnp.minimum(i + 1, nt - 1), 0, 0), memory_space=pltpu.SMEM),
                  row(D_MODEL), row(LANES), row(PLE_DIM), full(wp), full(wg), full(gp), full(gf),
                  pl.BlockSpec(memory_space=pl.ANY)],
        out_specs=row(D_MODEL),
        out_shape=jax.ShapeDtypeStruct((n, D_MODEL), F32),
        scratch_shapes=[pltpu.VMEM((2, 2) + _tiled(tm), F32), pltpu.SemaphoreType.DMA((2,))],
        compiler_params=_params(("arbitrary",)),
        name="combine_ple",
    )(dest, dest, h, gate_col, p, wp, wg, gp, gf, y_sorted)


def _pad_lanes(v, width=LANES):
    return jnp.zeros((1, width), F32).at[0, :v.shape[0]].set(v.astype(F32))


def _encoder(x3, p3, prm):
    batch, seq, _ = x3.shape
    n = batch * seq
    assert seq % SSD_TILE == 0 and (seq // GRID_W) % NA_ROWS == 0 and n % TOKEN_TILE == 0
    x = x3.reshape(n, D_MODEL)
    p = p3.reshape(n, PLE_DIM)

    z, xbc, q, k, v, dt = _inproj(x, prm["norm_mix"], prm["w_z"], prm["w_xbc"], prm["w_q"], prm["w_k"],
                                  prm["w_v"], prm["w_dt"])

    y_bwd, act = _ssd_pass(xbc, dt, prm["dt_bias"][1], prm["a_log"][1], batch=batch, seq=seq, backward=True,
                           extra=(prm["conv_w"], prm["conv_b"]))
    y_ssd = _ssd_pass(act, dt, prm["dt_bias"][0], prm["a_log"][0], batch=batch, seq=seq, backward=False,
                      extra=(z, y_bwd, prm["d_exp"], prm["ssd_norm"]))
    y_att = _natten(q, k, v, prm["bias_tab"], prm["attn_norm"], batch=batch, seq=seq)

    h1, u2, route, gate_col, counts = _outproj(x, y_ssd, y_att, prm["w_out_a"], prm["w_out_b"], prm["norm_ffn"],
                                               prm["w_router"], prm["upper"])

    cnt = counts[:, 0].astype(jnp.int32)
    padded = (cnt + SLOT_BLOCK - 1) // SLOT_BLOCK * SLOT_BLOCK
    pad_end = jnp.cumsum(padded)
    pad_start = pad_end - padded
    n_blocks = -(-(2 * n + N_EXPERTS * (SLOT_BLOCK - 1)) // SLOT_BLOCK)
    n_slots = n_blocks * SLOT_BLOCK
    first_slot = jnp.arange(n_blocks, dtype=jnp.int32) * SLOT_BLOCK
    blk_e = jnp.minimum(jnp.sum((pad_end[None, :] <= first_slot[:, None]).astype(jnp.int32), axis=1), N_EXPERTS - 1)
    used = pad_end[-1:] // SLOT_BLOCK
    hot = route[0:2][None] == jnp.arange(N_EXPERTS, dtype=jnp.int32)[:, None, None]
    dest = route[2:4] + jnp.sum(jnp.where(hot, pad_start[:, None, None], 0), axis=0)

    slots = _dispatch(u2, dest, pad_start, pad_end, n_slots)
    y_sorted = _experts(slots, blk_e, used, prm["w_gate"], prm["w_up"], prm["w_down"])
    out = _combine(dest, h1, gate_col, p, prm["w_ple_proj"], prm["w_ple_gate"], prm["norm_ple"],
                   prm["norm_final"], y_sorted)
    return out.reshape(batch, seq, D_MODEL)


def _prepare(w_in, conv_w, conv_b, dt_bias, a_log, d_skip, ssd_norm, rel_bias, attn_norm, w_out, norm_mix,
             norm_ffn, router_group, router_expert, w_gate, w_up, w_down, norm_ple, w_ple_proj, w_ple_gate,
             norm_final):
    w_in = w_in[0]
    offs = np.cumsum([0, SSD_WIDTH, CONV_DIM, 2 * N_HEADS, D_MODEL, D_MODEL, D_MODEL])
    piece = lambda i: w_in[:, offs[i]:offs[i + 1]]
    w_dt = jnp.zeros((D_MODEL, 2 * LANES), F32)
    w_dt = w_dt.at[:, :N_HEADS].set(piece(2)[:, :N_HEADS]).at[:, LANES:LANES + N_HEADS].set(piece(2)[:, N_HEADS:])
    row = lambda v: v.astype(F32).reshape(1, -1)
    w_router = jnp.zeros((ROUTER_ROWS, D_MODEL), F32)
    w_router = w_router.at[:N_GROUPS].set(router_group[0].T)
    w_router = w_router.at[8:8 + N_EXPERTS].set(router_expert[0].transpose(0, 2, 1).reshape(N_EXPERTS, D_MODEL))
    tm = TOKEN_TILE
    upper = jnp.asarray(np.triu(np.ones((tm, tm), np.float32), k=1), BF16)
    return {
        "norm_mix": row(norm_mix[0]),
        "w_z": piece(0).astype(BF16), "w_xbc": piece(1).astype(BF16), "w_dt": w_dt.astype(BF16),
        "w_q": piece(3).astype(BF16), "w_k": piece(4).astype(BF16), "w_v": piece(5).astype(BF16),
        "conv_w": jnp.zeros((8, CONV_DIM), F32).at[:D_CONV].set(conv_w[0]),
        "conv_b": row(conv_b[0]),
        "dt_bias": [_pad_lanes(dt_bias[0, d]) for d in range(2)],
        "a_log": [jnp.full((1, LANES), -jnp.inf, F32).at[0, :N_HEADS].set(a_log[0, d]) for d in range(2)],
        "d_exp": jnp.repeat(d_skip[0].astype(F32), HEAD_DIM).reshape(1, SSD_WIDTH),
        "ssd_norm": row(ssd_norm[0]),
        "bias_tab": _bias_table(rel_bias[0]),
        "attn_norm": row(attn_norm[0]),
        "w_out_a": w_out[0, :SSD_WIDTH].astype(BF16), "w_out_b": w_out[0, SSD_WIDTH:].astype(BF16),
        "norm_ffn": row(norm_ffn[0]),
        "w_router": w_router,
        "upper": upper,
        "w_gate": w_gate[0], "w_up": w_up[0], "w_down": w_down[0],
        "norm_ple": row(norm_ple[0]),
        "w_ple_proj": w_ple_proj[0].astype(BF16), "w_ple_gate": w_ple_gate[0].astype(BF16),
        "norm_final": row(norm_final),
    }


def kernel(x_prompt, x_sample, p_prompt, p_sample, w_in, conv_w, conv_b, dt_bias, a_log, d_skip, ssd_norm,
           rel_bias, attn_norm, w_out, norm_mix, norm_ffn, router_group, router_expert, w_gate, w_up, w_down,
           norm_ple, w_ple_proj, w_ple_gate, norm_final):
    prm = _prepare(w_in, conv_w, conv_b, dt_bias, a_log, d_skip, ssd_norm, rel_bias, attn_norm, w_out, norm_mix,
                   norm_ffn, router_group, router_expert, w_gate, w_up, w_down, norm_ple, w_ple_proj, w_ple_gate,
                   norm_final)
    y_prompt = _encoder(x_prompt, p_prompt[0], prm)
    y_sample = _encoder(x_sample, p_sample[0], prm)
    return (y_prompt, y_sample)
```

```python
import functools
import math

import numpy as np
import jax
import jax.numpy as jnp
from jax import lax
from jax.experimental import pallas as pl
from jax.experimental.pallas import tpu as pltpu

F32 = jnp.float32
BF16 = jnp.bfloat16

D_MODEL = 1024
SSD_WIDTH = 1024
HEAD_DIM = 64
N_HEADS = 16
N_BC_GROUPS = 2
D_STATE = 128
D_CONV = 5
CHUNK = 128
CONV_DIM = SSD_WIDTH + 2 * N_BC_GROUPS * D_STATE
HEADS_PER_GROUP = N_HEADS // N_BC_GROUPS
GROUP_WIDTH = SSD_WIDTH // N_BC_GROUPS
GRID_W = 64
WIN_H = 8
WIN_W = 16
N_GROUPS = 4
EXPERTS_PER_GROUP = 8
N_EXPERTS = N_GROUPS * EXPERTS_PER_GROUP
D_EXPERT = 512
PLE_DIM = 256
EPS = 1e-6
LOG2E = math.log2(math.e)

LANES = 128
HALO_ROWS = 16
TOKEN_TILE = 512
SSD_TILE = 1024
CONV_COLS = 512
SSD_TRIP_CHUNKS = 8
NA_ROWS = 16
NA_HALO_ROWS = 4
NA_TRIP_ROWS = 8
SLOT_BLOCK = 512
DISPATCH_TILE = 1024
COMBINE_TILE = 512
ROW_GROUP = 8
ROW_SUB = 8
ROUTER_ROWS = 48
VMEM_LIMIT = 56 * 1024 * 1024


def _rms(x, g):
    ms = jnp.mean(x * x, axis=-1, keepdims=True)
    return x * lax.rsqrt(ms + EPS) * g


def _split3(x):
    hi = x.astype(BF16)
    r1 = x - hi.astype(F32)
    mid = r1.astype(BF16)
    lo = (r1 - mid.astype(F32)).astype(BF16)
    return hi, mid, lo


def _dot(a, b):
    return jnp.dot(a, b, preferred_element_type=F32)


def _dot_nt(a, b):
    return lax.dot_general(a, b, (((1,), (1,)), ((), ())), preferred_element_type=F32)


def _tiled(m):
    return (m * ROW_SUB, LANES)


def _dot_tn(a, b):
    return lax.dot_general(a, b, (((0,), (0,)), ((), ())), preferred_element_type=F32)


def _store_row_tiles(ref, x):
    m = x.shape[0]
    for s in range(ROW_SUB):
        ref[pl.ds(s, m, stride=ROW_SUB), :] = x[:, s * LANES:(s + 1) * LANES]


def _load_row_tiles(ref):
    m = ref.shape[0] // ROW_SUB
    return jnp.concatenate([ref[pl.ds(s, m, stride=ROW_SUB), :] for s in range(ROW_SUB)], axis=1)


def _params(sem):
    return pltpu.CompilerParams(dimension_semantics=sem, vmem_limit_bytes=VMEM_LIMIT)


def _inproj_kernel(x_ref, g_ref, wz_ref, wxbc_ref, wq_ref, wk_ref, wv_ref, wdt_ref,
                   z_ref, xbc_ref, q_ref, k_ref, v_ref, dt_ref):
    u = _rms(x_ref[...], g_ref[...]).astype(BF16)
    z_ref[...] = _dot(u, wz_ref[...]).astype(BF16)
    xbc_ref[...] = _dot(u, wxbc_ref[...]).astype(BF16)
    q_ref[...] = (_dot(u, wq_ref[...]) * (HEAD_DIM ** -0.5 * LOG2E)).astype(BF16)
    k_ref[...] = _dot(u, wk_ref[...]).astype(BF16)
    v_ref[...] = _dot(u, wv_ref[...]).astype(BF16)
    dt_ref[...] = _dot(u, wdt_ref[...])


def _inproj(x, g, wz, wxbc, wq, wk, wv, wdt):
    n = x.shape[0]
    tm = TOKEN_TILE
    row = lambda w: pl.BlockSpec((tm, w), lambda i: (i, 0))
    full = lambda a: pl.BlockSpec(a.shape, lambda i: (0,) * a.ndim)
    outs = [(D_MODEL, BF16), (CONV_DIM, BF16), (D_MODEL, BF16), (D_MODEL, BF16), (D_MODEL, BF16),
            (2 * LANES, F32)]
    return pl.pallas_call(
        _inproj_kernel,
        grid=(n // tm,),
        in_specs=[row(D_MODEL), full(g), full(wz), full(wxbc), full(wq), full(wk), full(wv), full(wdt)],
        out_specs=[row(w) for w, _ in outs],
        out_shape=[jax.ShapeDtypeStruct((n, w), d) for w, d in outs],
        compiler_params=_params(("parallel",)),
        name="inproj",
    )(x, g, wz, wxbc, wq, wk, wv, wdt)


def _ssd_kernel(*refs, backward, n_blocks):
    if backward:
        (xbc_ref, prev_ref, next_ref, dt_ref, cw_ref, cb_ref, dtb_ref, alog_ref,
         out_ref, act_ref, xp_ref, st_ref) = refs
    else:
        (act_ref, dt_ref, dtb_ref, alog_ref, z_ref, ybwd_ref, dexp_ref, gn_ref, out_ref, st_ref) = refs
    tb = act_ref.shape[0]
    t = CHUNK
    n_chunks = tb // t
    j = pl.program_id(1)

    @pl.when(j == 0)
    def _():
        st_ref[...] = jnp.zeros_like(st_ref)

    if backward:
        blk = n_blocks - 1 - j
        hr = HALO_ROWS
        xp_ref[0:hr, :] = jnp.where(blk == 0, jnp.zeros_like(prev_ref), prev_ref[...])
        xp_ref[hr:hr + tb, :] = xbc_ref[...]
        xp_ref[hr + tb:, :] = jnp.where(blk == n_blocks - 1, jnp.zeros_like(next_ref), next_ref[...])
        taps = [kk for kk in range(D_CONV) if kk != D_CONV // 2]
        win = t + 2 * hr
        srow = lax.broadcasted_iota(jnp.int32, (len(taps) * t, win), 0)
        scol = lax.broadcasted_iota(jnp.int32, (len(taps) * t, win), 1)
        tap_id = srow // t
        tap_off = tap_id + jnp.where(tap_id >= D_CONV // 2, 1, 0) + (hr - D_CONV // 2)
        shift_sel = jnp.where(scol == (srow % t) + tap_off, 1.0, 0.0).astype(BF16)
        for c in range(n_chunks):
            for c0 in range(0, CONV_DIM, CONV_COLS):
                cols = slice(c0, c0 + CONV_COLS)
                window = xp_ref[c * t:c * t + win, cols]
                shifted = _dot(shift_sel, window)
                acc = cb_ref[:, cols] + cw_ref[D_CONV // 2:D_CONV // 2 + 1, cols] * window[hr:hr + t].astype(F32)
                for ti, kk in enumerate(taps):
                    acc = acc + cw_ref[kk:kk + 1, cols] * shifted[ti * t:(ti + 1) * t]
                act_ref[c * t:(c + 1) * t, cols] = (acc * (1.0 / (1.0 + jnp.exp(-acc)))).astype(act_ref.dtype)

    a_row = -jnp.exp(alog_ref[...]) * LOG2E
    dtb = dtb_ref[...]
    ri = lax.broadcasted_iota(jnp.int32, (t, t), 0)
    ci = lax.broadcasted_iota(jnp.int32, (t, t), 1)
    if backward:
        keep = ci >= ri
    else:
        keep = ri >= ci
    tri = jnp.where(keep, 1.0, 0.0).astype(BF16)
    er = lax.broadcasted_iota(jnp.int32, (2 * LANES, SSD_WIDTH), 0)
    ec = lax.broadcasted_iota(jnp.int32, (2 * LANES, SSD_WIDTH), 1)
    expand = jnp.where((er % LANES) == (ec // HEAD_DIM), 1.0, 0.0).astype(BF16)
    lane = lax.broadcasted_iota(jnp.int32, (t, LANES), 1)
    low_half = lane < HEAD_DIM

    def chunk(c, st):
        r0 = pl.multiple_of(c * t, t)
        act = act_ref[pl.ds(r0, t), :]
        xs_bf = act[:, :SSD_WIDTH]
        xs = xs_bf.astype(F32)
        groups = []
        for g in range(N_BC_GROUPS):
            b_g = act[:, SSD_WIDTH + g * D_STATE:SSD_WIDTH + (g + 1) * D_STATE]
            c_g = act[:, SSD_WIDTH + (N_BC_GROUPS + g) * D_STATE:SSD_WIDTH + (N_BC_GROUPS + g + 1) * D_STATE]
            groups.append((b_g, c_g, slice(g * GROUP_WIDTH, (g + 1) * GROUP_WIDTH)))

        st_bf = st.astype(BF16)
        cbs = [_dot_nt(c_g, b_g) for b_g, c_g, _ in groups]
        y_offs = [_dot(c_g, st_bf[:, hs]) for _, c_g, hs in groups]

        dt_raw = dt_ref[pl.ds(r0, t), :] + dtb
        dt = jnp.maximum(dt_raw, 0.0) + jnp.log(1.0 + jnp.exp(-jnp.abs(dt_raw)))
        hi, mid, lo = _split3(dt * a_row)
        cs = _dot(tri, hi) + _dot(tri, mid) + _dot(tri, lo)
        cs_row = cs.T
        dt_row = dt.T
        edge = cs[0:1, :] if backward else cs[t - 1:t, :]
        w_in = dt * jnp.exp2(edge - cs)
        ecs = jnp.exp2(cs)
        both = jnp.concatenate([w_in, ecs], axis=0)
        b_hi = both.astype(BF16)
        b_lo = (both - b_hi.astype(F32)).astype(BF16)
        both_x = _dot(jnp.concatenate([b_hi, b_lo], axis=1), expand)
        w_x = both_x[:t]
        ecs_x = both_x[t:]
        cd_x = ecs_x[0:1, :] if backward else ecs_x[t - 1:t, :]

        y_parts = []
        for g, (_, _, hs) in enumerate(groups):
            pair_parts = []
            for pp in range(HEADS_PER_GROUP // 2):
                h0 = g * HEADS_PER_GROUP + 2 * pp
                xs_pair = xs_bf[:, h0 * HEAD_DIM:(h0 + 2) * HEAD_DIM]
                ys = []
                for h in (h0, h0 + 1):
                    diff = cs[:, h:h + 1] - cs_row[h:h + 1, :]
                    seg = jnp.exp2(jnp.where(keep, diff, -jnp.inf))
                    m = (cbs[g] * seg * dt_row[h:h + 1, :]).astype(BF16)
                    ys.append(_dot(m, xs_pair))
                pair_parts.append(jnp.where(low_half, ys[0], ys[1]))
            y_parts.append(jnp.concatenate(pair_parts, axis=1) + y_offs[g] * ecs_x[:, hs])
        y = jnp.concatenate(y_parts, axis=1)

        xw = (xs * w_x).astype(BF16)
        news = [_dot_tn(b_g, xw[:, hs]) for b_g, _, hs in groups]
        st_new = st * cd_x + jnp.concatenate(news, axis=1)

        if backward:
            out_ref[pl.ds(r0, t), :] = y.astype(out_ref.dtype)
        else:
            y = y + ybwd_ref[pl.ds(r0, t), :].astype(F32) + dexp_ref[...] * xs
            zz = z_ref[pl.ds(r0, t), :].astype(F32)
            y = y * (zz * (1.0 / (1.0 + jnp.exp(-zz))))
            out_ref[pl.ds(r0, t), :] = _rms(y, gn_ref[...]).astype(out_ref.dtype)
        return st_new

    def trip(i, carry):
        st = st_ref[...]
        for u in range(SSD_TRIP_CHUNKS):
            ci = i * SSD_TRIP_CHUNKS + u
            st = chunk((n_chunks - 1 - ci) if backward else ci, st)
        st_ref[...] = st
        return carry

    lax.fori_loop(0, n_chunks // SSD_TRIP_CHUNKS, trip, 0)


def _ssd_pass(xin, dt, dtb, alog, *, batch, seq, backward, extra=()):
    n = xin.shape[0]
    tb = SSD_TILE
    nb = seq // tb
    hb = tb // HALO_ROWS
    n_halo = n // HALO_ROWS
    d = 1 if backward else 0

    def blk_of(b, j):
        return b * nb + ((nb - 1 - j) if backward else j)

    main = lambda w: pl.BlockSpec((tb, w), lambda b, j: (blk_of(b, j), 0))
    full = lambda a: pl.BlockSpec(a.shape, lambda b, j: (0,) * a.ndim)
    dt_spec = pl.BlockSpec((tb, LANES), lambda b, j: (blk_of(b, j), d))
    state = pltpu.VMEM((D_STATE, SSD_WIDTH), F32)
    y_shape = jax.ShapeDtypeStruct((n, SSD_WIDTH), BF16)
    if backward:
        cw, cb = extra
        in_specs = [
            main(CONV_DIM),
            pl.BlockSpec((HALO_ROWS, CONV_DIM), lambda b, j: (jnp.maximum(blk_of(b, j) * hb - 1, 0), 0)),
            pl.BlockSpec((HALO_ROWS, CONV_DIM), lambda b, j: (jnp.minimum((blk_of(b, j) + 1) * hb, n_halo - 1), 0)),
            dt_spec, full(cw), full(cb), full(dtb), full(alog),
        ]
        args = [xin, xin, xin, dt, cw, cb, dtb, alog]
        out_specs = [main(SSD_WIDTH), main(CONV_DIM)]
        out_shape = [y_shape, jax.ShapeDtypeStruct((n, CONV_DIM), BF16)]
        scratch = [pltpu.VMEM((tb + 2 * HALO_ROWS, CONV_DIM), BF16), state]
    else:
        z, ybwd, dexp, gn = extra
        in_specs = [main(CONV_DIM), dt_spec, full(dtb), full(alog), main(SSD_WIDTH), main(SSD_WIDTH), full(dexp),
                    full(gn)]
        args = [xin, dt, dtb, alog, z, ybwd, dexp, gn]
        out_specs = main(SSD_WIDTH)
        out_shape = y_shape
        scratch = [state]
    return pl.pallas_call(
        functools.partial(_ssd_kernel, backward=backward, n_blocks=nb),
        grid=(batch, nb),
        in_specs=in_specs,
        out_specs=out_specs,
        out_shape=out_shape,
        scratch_shapes=scratch,
        compiler_params=_params(("parallel", "arbitrary")),
        name="ssd_bwd" if backward else "ssd_fwd",
    )(*args)


def _natten_kernel(q_ref, kp_ref, kc_ref, kn_ref, vp_ref, vc_ref, vn_ref, bias_ref, gn_ref,
                   out_ref, kbuf, vbuf, obuf, *, rows):
    halo = NA_HALO_ROWS * GRID_W
    full = NA_ROWS * GRID_W
    jb = pl.program_id(1)
    kbuf[0:halo, :] = kp_ref[...]
    kbuf[halo:halo + full, :] = kc_ref[...]
    kbuf[halo + full:, :] = kn_ref[...]
    vbuf[0:halo, :] = vp_ref[...]
    vbuf[halo:halo + full, :] = vc_ref[...]
    vbuf[halo + full:, :] = vn_ref[...]
    lane = lax.broadcasted_iota(jnp.int32, (GRID_W, LANES), 1)
    low_half = lane < HEAD_DIM
    n_keys = WIN_H * GRID_W
    ones_keys = jnp.ones((n_keys, LANES), BF16)

    n_pairs = N_HEADS // 2
    lanes_of = lambda pp: slice(pp * LANES, (pp + 1) * LANES)

    def trip_body(it):
        geo = []
        for rr in range(NA_TRIP_ROWS):
            i = it * NA_TRIP_ROWS + rr
            r = jb * NA_ROWS + i
            s = jnp.clip(r - WIN_H // 2, 0, rows - WIN_H)
            local = s - (jb * NA_ROWS - NA_HALO_ROWS)
            geo.append((pl.multiple_of(local * GRID_W, GRID_W), pl.multiple_of(i * GRID_W, GRID_W),
                        s - r + (WIN_H - 1)))
        scores = []
        for k0, q0, d0 in geo:
            for pp in range(n_pairs):
                q_pair = q_ref[pl.ds(q0, GRID_W), lanes_of(pp)]
                zero = jnp.zeros_like(q_pair)
                q2 = jnp.concatenate([jnp.where(low_half, q_pair, zero), jnp.where(low_half, zero, q_pair)], axis=0)
                sc = _dot_nt(q2, kbuf[pl.ds(k0, n_keys), lanes_of(pp)])
                bias = jnp.concatenate(
                    [jnp.concatenate([bias_ref[d0 + 2 * ii, 2 * pp + hh] for ii in range(WIN_H // 2)], axis=1)
                     for hh in range(2)], axis=0)
                scores.append(sc + bias)
        probs = []
        for sc in scores:
            m = jnp.max(sc, axis=-1, keepdims=True)
            probs.append(jnp.exp2(sc - m).astype(BF16))
        for rr, (k0, q0, d0) in enumerate(geo):
            for pp in range(n_pairs):
                v_ext = jnp.concatenate([vbuf[pl.ds(k0, n_keys), lanes_of(pp)], ones_keys], axis=1)
                o_ext = _dot(probs[rr * n_pairs + pp], v_ext)
                o2 = o_ext[:, :LANES] * (1.0 / o_ext[:, LANES:])
                obuf[pl.ds(q0, GRID_W), lanes_of(pp)] = jnp.where(low_half, o2[:GRID_W], o2[GRID_W:])
        for rr, (k0, q0, d0) in enumerate(geo):
            out_ref[pl.ds(q0, GRID_W), :] = _rms(obuf[pl.ds(q0, GRID_W), :], gn_ref[...]).astype(out_ref.dtype)

    for it in range(NA_ROWS // NA_TRIP_ROWS):
        trip_body(it)


def _natten(q, k, v, bias_tab, gn, *, batch, seq):
    n = q.shape[0]
    rows = seq // GRID_W
    nrb = rows // NA_ROWS
    full = NA_ROWS * GRID_W
    halo = NA_HALO_ROWS * GRID_W
    per = NA_ROWS // NA_HALO_ROWS
    n_halo = n // halo
    cur = pl.BlockSpec((full, D_MODEL), lambda b, j: (b * nrb + j, 0))
    prev = pl.BlockSpec((halo, D_MODEL), lambda b, j: (jnp.maximum((b * nrb + j) * per - 1, 0), 0))
    nxt = pl.BlockSpec((halo, D_MODEL), lambda b, j: (jnp.minimum((b * nrb + j + 1) * per, n_halo - 1), 0))
    const = lambda a: pl.BlockSpec(a.shape, lambda b, j: (0,) * a.ndim)
    return pl.pallas_call(
        functools.partial(_natten_kernel, rows=rows),
        grid=(batch, nrb),
        in_specs=[cur, prev, cur, nxt, prev, cur, nxt, const(bias_tab), const(gn)],
        out_specs=cur,
        out_shape=jax.ShapeDtypeStruct((n, D_MODEL), BF16),
        scratch_shapes=[pltpu.VMEM((full + 2 * halo, D_MODEL), BF16), pltpu.VMEM((full + 2 * halo, D_MODEL), BF16),
                        pltpu.VMEM((full, D_MODEL), F32)],
        compiler_params=_params(("parallel", "parallel")),
        name="natten",
    )(q, k, k, k, v, v, v, bias_tab, gn)


def _bias_table(rel_bias):
    qc = np.arange(GRID_W)[:, None]
    kc = np.arange(GRID_W)[None, :]
    s_c = np.clip(qc - WIN_W // 2, 0, GRID_W - WIN_W)
    inside = (kc >= s_c) & (kc < s_c + WIN_W)
    dcol = np.clip(kc - qc, -(WIN_W - 1), WIN_W - 1) + WIN_W - 1
    tab = rel_bias.astype(F32)[:, :, dcol]
    tab = jnp.where(inside[None, None], tab * LOG2E, -jnp.inf)
    pairs = jnp.concatenate([tab[:, :-1], tab[:, 1:]], axis=-1)
    return pairs.transpose(1, 0, 2, 3)


def _outproj_kernel(x_ref, ys_ref, ya_ref, wa_ref, wb_ref, gn_ref, wr_ref, upper_ref,
                    h_ref, u_ref, idx_ref, gate_ref, cnt_ref, carry_ref):
    tm = x_ref.shape[0]

    @pl.when(pl.program_id(0) == 0)
    def _():
        carry_ref[...] = jnp.zeros_like(carry_ref)

    h = x_ref[...] + _dot(ys_ref[...], wa_ref[...]) + _dot(ya_ref[...], wb_ref[...])
    h_ref[...] = h
    u = _rms(h, gn_ref[...])
    _store_row_tiles(u_ref, u)

    u_hi = u.astype(BF16)
    u_lo = (u - u_hi.astype(F32)).astype(BF16)
    wr = wr_ref[...]
    w_hi = wr.astype(BF16)
    w_lo = (wr - w_hi.astype(F32)).astype(BF16)
    both = _dot_nt(jnp.concatenate([w_hi, w_lo], axis=0), u_hi)
    logit = both[:ROUTER_ROWS] + both[ROUTER_ROWS:] + _dot_nt(w_hi, u_lo)

    lg = logit[0:N_GROUPS]
    gi = lax.broadcasted_iota(jnp.int32, (N_GROUPS, tm), 0)
    g_max = jnp.max(lg, axis=0, keepdims=True)
    g_idx = jnp.min(jnp.where(lg == g_max, gi, N_GROUPS), axis=0, keepdims=True)
    g_val = 1.0 / jnp.sum(jnp.exp(lg - g_max), axis=0, keepdims=True)

    le = logit[8:8 + EXPERTS_PER_GROUP]
    for g in range(1, N_GROUPS):
        le = jnp.where(g_idx == g, logit[8 + g * EXPERTS_PER_GROUP:8 + (g + 1) * EXPERTS_PER_GROUP], le)
    ei = lax.broadcasted_iota(jnp.int32, (EXPERTS_PER_GROUP, tm), 0)
    m1 = jnp.max(le, axis=0, keepdims=True)
    i1 = jnp.min(jnp.where(le == m1, ei, EXPERTS_PER_GROUP), axis=0, keepdims=True)
    le2 = jnp.where(ei == i1, -jnp.inf, le)
    m2 = jnp.max(le2, axis=0, keepdims=True)
    i2 = jnp.min(jnp.where(le2 == m2, ei, EXPERTS_PER_GROUP), axis=0, keepdims=True)
    e21 = jnp.exp(m2 - m1)
    den = 1.0 / (1.0 + e21)
    gate1 = g_val * den
    gate2 = g_val * e21 * den
    e1 = g_idx * EXPERTS_PER_GROUP + i1
    e2 = g_idx * EXPERTS_PER_GROUP + i2

    xi = lax.broadcasted_iota(jnp.int32, (N_EXPERTS, tm), 0)
    hot1 = xi == e1
    hot2 = xi == e2
    hot = jnp.where(jnp.logical_or(hot1, hot2), 1.0, 0.0)
    before = _dot(hot.astype(BF16), upper_ref[...])
    carry = carry_ref[...]
    before = before + jnp.tile(carry, (1, tm // LANES))
    rank1 = jnp.sum(jnp.where(hot1, before, 0.0), axis=0, keepdims=True)
    rank2 = jnp.sum(jnp.where(hot2, before, 0.0), axis=0, keepdims=True)
    carry = carry + jnp.sum(hot, axis=1, keepdims=True)
    carry_ref[...] = carry
    cnt_ref[...] = carry

    zi = jnp.zeros((4, tm), jnp.int32)
    idx_ref[...] = jnp.concatenate([e1, e2, rank1.astype(jnp.int32), rank2.astype(jnp.int32), zi], axis=0)
    gates = jnp.concatenate([gate1, gate2, jnp.zeros((LANES - 2, tm), F32)], axis=0)
    gate_ref[...] = gates.T


def _outproj(x, y_ssd, y_att, wa, wb, gn, wr, upper):
    n = x.shape[0]
    tm = TOKEN_TILE
    row = lambda w: pl.BlockSpec((tm, w), lambda i: (i, 0))
    full = lambda a: pl.BlockSpec(a.shape, lambda i: (0,) * a.ndim)
    return pl.pallas_call(
        _outproj_kernel,
        grid=(n // tm,),
        in_specs=[row(D_MODEL), row(D_MODEL), row(D_MODEL), full(wa), full(wb), full(gn), full(wr), full(upper)],
        out_specs=[row(D_MODEL), pl.BlockSpec(_tiled(tm), lambda i: (i, 0)),
                   pl.BlockSpec((8, tm), lambda i: (0, i)), row(LANES),
                   pl.BlockSpec((N_EXPERTS, LANES), lambda i: (0, 0))],
        out_shape=[jax.ShapeDtypeStruct((n, D_MODEL), F32), jax.ShapeDtypeStruct(_tiled(n), F32),
                   jax.ShapeDtypeStruct((8, n), jnp.int32), jax.ShapeDtypeStruct((n, LANES), F32),
                   jax.ShapeDtypeStruct((N_EXPERTS, LANES), F32)],
        scratch_shapes=[pltpu.VMEM((N_EXPERTS, LANES), F32)],
        compiler_params=_params(("arbitrary",)),
        name="outproj_router",
    )(x, y_ssd, y_att, wa, wb, gn, wr, upper)


def _rows(ref, first, count):
    return ref.at[pl.ds(pl.multiple_of(first * ROW_SUB, ROW_SUB), count * ROW_SUB)]


def _row_copy(src_ref, src_row, dst_ref, dst_row, sem):
    return pltpu.make_async_copy(_rows(src_ref, src_row, 1), _rows(dst_ref, dst_row, 1), sem)


def _for_each_choice(dest_ref, n_rows, fn, unrolled=False):
    if unrolled:
        for r in range(n_rows):
            for kk in range(2):
                fn(r, kk, dest_ref[0, 0, kk * n_rows + r])
        return

    def group(g, carry):
        base = pl.multiple_of(g * ROW_GROUP, ROW_GROUP)
        for jj in range(ROW_GROUP):
            for kk in range(2):
                fn(base + jj, kk, dest_ref[0, 0, kk * n_rows + base + jj])
        return carry

    lax.fori_loop(0, n_rows // ROW_GROUP, group, 0)


def _dispatch_kernel(pstart_ref, pend_ref, dest_ref, u_ref, slots_ref, zero_ref, sem):
    tm = u_ref.shape[0] // ROW_SUB

    @pl.when(pl.program_id(0) == 0)
    def _():
        zero_ref[...] = jnp.zeros_like(zero_ref)

        def fill(e):
            return pltpu.make_async_copy(zero_ref, _rows(slots_ref, pend_ref[e] - SLOT_BLOCK, SLOT_BLOCK), sem)

        for e in range(N_EXPERTS):
            @pl.when(pend_ref[e] > pstart_ref[e])
            def _():
                fill(e).start()
        for e in range(N_EXPERTS):
            @pl.when(pend_ref[e] > pstart_ref[e])
            def _():
                fill(e).wait()

        def tail(j):
            return pltpu.make_async_copy(zero_ref, _rows(slots_ref, j * SLOT_BLOCK, SLOT_BLOCK), sem)

        used = lax.div(pend_ref[N_EXPERTS - 1], SLOT_BLOCK)
        n_blocks = slots_ref.shape[0] // (SLOT_BLOCK * ROW_SUB)

        def tail_start(j, carry):
            tail(j).start()
            return carry

        def tail_wait(j, carry):
            tail(j).wait()
            return carry

        lax.fori_loop(used, n_blocks, tail_start, 0)
        lax.fori_loop(used, n_blocks, tail_wait, 0)

    _for_each_choice(dest_ref, tm, lambda r, kk, slot: _row_copy(u_ref, r, slots_ref, slot, sem).start(priority=kk))
    for _ in range(2):
        pltpu.make_async_copy(u_ref, _rows(slots_ref, 0, tm), sem).wait()


def _tile_slots(dest, tm):
    nt = dest.shape[1] // tm
    return dest.reshape(2, nt, tm).transpose(1, 0, 2).reshape(nt, 1, 2 * tm)


def _dispatch(u, dest, pad_start, pad_end, n_slots):
    n = u.shape[0] // ROW_SUB
    tm = DISPATCH_TILE
    dest = _tile_slots(dest, tm)
    grid_spec = pltpu.PrefetchScalarGridSpec(
        num_scalar_prefetch=2,
        grid=(n // tm,),
        in_specs=[pl.BlockSpec((1, 1, 2 * tm), lambda i, ps, pe: (i, 0, 0), memory_space=pltpu.SMEM),
                  pl.BlockSpec(_tiled(tm), lambda i, ps, pe: (i, 0))],
        out_specs=pl.BlockSpec(memory_space=pl.ANY),
        scratch_shapes=[pltpu.VMEM(_tiled(SLOT_BLOCK), F32), pltpu.SemaphoreType.DMA(())],
    )
    return pl.pallas_call(
        _dispatch_kernel,
        grid_spec=grid_spec,
        out_shape=jax.ShapeDtypeStruct(_tiled(n_slots), F32),
        compiler_params=_params(("arbitrary",)),
        name="dispatch",
    )(pad_start, pad_end, dest, u)


def _expert_kernel(blk_e_ref, used_ref, x_ref, wg_ref, wu_ref, wd_ref, y_ref, wg_bf, wu_bf, wd_bf):
    j = pl.program_id(0)

    @pl.when(jnp.logical_or(j == 0, blk_e_ref[j] != blk_e_ref[jnp.maximum(j - 1, 0)]))
    def _():
        wg_bf[...] = wg_ref[0].astype(BF16)
        wu_bf[...] = wu_ref[0].astype(BF16)
        wd_bf[...] = wd_ref[0].astype(BF16)

    @pl.when(j < used_ref[0])
    def _():
        x = _load_row_tiles(x_ref).astype(BF16)
        a = _dot(x, wg_bf[...])
        b = _dot(x, wu_bf[...])
        hid = (a * (1.0 / (1.0 + jnp.exp(-a))) * b).astype(BF16)
        _store_row_tiles(y_ref, _dot(hid, wd_bf[...]))

    @pl.when(pl.program_id(0) >= used_ref[0])
    def _():
        y_ref[...] = jnp.zeros_like(y_ref)


def _experts(slots, blk_e, used, wg, wu, wd):
    n_slots = slots.shape[0] // ROW_SUB
    nb = n_slots // SLOT_BLOCK
    xmap = lambda i, be, us: (jnp.minimum(i, us[0] - 1), 0)
    wmap = lambda i, be, us: (be[i], 0, 0)
    grid_spec = pltpu.PrefetchScalarGridSpec(
        num_scalar_prefetch=2,
        grid=(nb,),
        in_specs=[pl.BlockSpec(_tiled(SLOT_BLOCK), xmap),
                  pl.BlockSpec((1, D_MODEL, D_EXPERT), wmap),
                  pl.BlockSpec((1, D_MODEL, D_EXPERT), wmap),
                  pl.BlockSpec((1, D_EXPERT, D_MODEL), wmap)],
        out_specs=pl.BlockSpec(_tiled(SLOT_BLOCK), lambda i, be, us: (i, 0)),
        scratch_shapes=[pltpu.VMEM((D_MODEL, D_EXPERT), BF16), pltpu.VMEM((D_MODEL, D_EXPERT), BF16),
                        pltpu.VMEM((D_EXPERT, D_MODEL), BF16)],
    )
    return pl.pallas_call(
        _expert_kernel,
        grid_spec=grid_spec,
        out_shape=jax.ShapeDtypeStruct(_tiled(n_slots), F32),
        compiler_params=_params(("arbitrary",)),
        name="experts",
    )(blk_e, used, slots, wg, wu, wd)


def _combine_kernel(dest_ref, dest_next_ref, h_ref, gate_ref, p_ref, wp_ref, wg_ref, gp_ref,
                    gf_ref, y_ref, out_ref, buf, sems):
    tm = h_ref.shape[0]
    i = pl.program_id(0)

    def gather(dref, which, unrolled):
        _for_each_choice(dref, tm, lambda r, kk, slot: _row_copy(y_ref, slot, buf.at[which, kk], r,
                                                                 sems.at[which]).start(priority=kk), unrolled)

    def wait_tile(which):
        for kk in range(2):
            pltpu.make_async_copy(_rows(y_ref, 0, tm), buf.at[which, kk], sems.at[which]).wait()

    @pl.when(i == 0)
    def _():
        gather(dest_ref, 0, False)

    cur = i % 2
    wait_tile(cur)
    gather(dest_next_ref, 1 - cur, True)

    gates = gate_ref[...]
    moe = gates[:, 0:1] * _load_row_tiles(buf.at[cur, 0]) + gates[:, 1:2] * _load_row_tiles(buf.at[cur, 1])
    h = h_ref[...] + moe
    u = _rms(h, gp_ref[...]).astype(BF16)
    gate = 1.0 / (1.0 + jnp.exp(-_dot(u, wg_ref[...])))
    h = h + _dot(p_ref[...].astype(BF16), wp_ref[...]) * gate
    out_ref[...] = _rms(h, gf_ref[...])

    @pl.when(i + 1 == pl.num_programs(0))
    def _():
        wait_tile(1 - cur)


def _combine(dest, h, gate_col, p, wp, wg, gp, gf, y_sorted):
    n = h.shape[0]
    tm = COMBINE_TILE
    nt = n // tm
    dest = _tile_slots(dest, tm)
    row = lambda w: pl.BlockSpec((tm, w), lambda i: (i, 0))
    full = lambda a: pl.BlockSpec(a.shape, lambda i: (0,) * a.ndim)
    return pl.pallas_call(
        _combine_kernel,
        grid=(nt,),
        in_specs=[pl.BlockSpec((1, 1, 2 * tm), lambda i: (i, 0, 0), memory_space=pltpu.SMEM),
                  pl.BlockSpec((1, 1, 2 * tm), lambda i: (jnp.minimum(i + 1, nt - 1), 0, 0), memory_space=pltpu.SMEM),
                  row(D_MODEL), row(LANES), row(PLE_DIM), full(wp), full(wg), full(gp), full(gf),
                  pl.BlockSpec(memory_space=pl.ANY)],
        out_specs=row(D_MODEL),
        out_shape=jax.ShapeDtypeStruct((n, D_MODEL), F32),
        scratch_shapes=[pltpu.VMEM((2, 2) + _tiled(tm), F32), pltpu.SemaphoreType.DMA((2,))],
        compiler_params=_params(("arbitrary",)),
        name="combine_ple",
    )(dest, dest, h, gate_col, p, wp, wg, gp, gf, y_sorted)


def _pad_lanes(v, width=LANES):
    return jnp.zeros((1, width), F32).at[0, :v.shape[0]].set(v.astype(F32))


def _encoder(x3, p3, prm):
    batch, seq, _ = x3.shape
    n = batch * seq
    assert seq % SSD_TILE == 0 and (seq // GRID_W) % NA_ROWS == 0 and n % TOKEN_TILE == 0
    x = x3.reshape(n, D_MODEL)
    p = p3.reshape(n, PLE_DIM)

    z, xbc, q, k, v, dt = _inproj(x, prm["norm_mix"], prm["w_z"], prm["w_xbc"], prm["w_q"], prm["w_k"],
                                  prm["w_v"], prm["w_dt"])

    y_bwd, act = _ssd_pass(xbc, dt, prm["dt_bias"][1], prm["a_log"][1], batch=batch, seq=seq, backward=True,
                           extra=(prm["conv_w"], prm["conv_b"]))
    y_ssd = _ssd_pass(act, dt, prm["dt_bias"][0], prm["a_log"][0], batch=batch, seq=seq, backward=False,
                      extra=(z, y_bwd, prm["d_exp"], prm["ssd_norm"]))
    y_att = _natten(q, k, v, prm["bias_tab"], prm["attn_norm"], batch=batch, seq=seq)

    h1, u2, route, gate_col, counts = _outproj(x, y_ssd, y_att, prm["w_out_a"], prm["w_out_b"], prm["norm_ffn"],
                                               prm["w_router"], prm["upper"])

    cnt = counts[:, 0].astype(jnp.int32)
    padded = (cnt + SLOT_BLOCK - 1) // SLOT_BLOCK * SLOT_BLOCK
    pad_end = jnp.cumsum(padded)
    pad_start = pad_end - padded
    n_blocks = -(-(2 * n + N_EXPERTS * (SLOT_BLOCK - 1)) // SLOT_BLOCK)
    n_slots = n_blocks * SLOT_BLOCK
    first_slot = jnp.arange(n_blocks, dtype=jnp.int32) * SLOT_BLOCK
    blk_e = jnp.minimum(jnp.sum((pad_end[None, :] <= first_slot[:, None]).astype(jnp.int32), axis=1), N_EXPERTS - 1)
    used = pad_end[-1:] // SLOT_BLOCK
    hot = route[0:2][None] == jnp.arange(N_EXPERTS, dtype=jnp.int32)[:, None, None]
    dest = route[2:4] + jnp.sum(jnp.where(hot, pad_start[:, None, None], 0), axis=0)

    slots = _dispatch(u2, dest, pad_start, pad_end, n_slots)
    y_sorted = _experts(slots, blk_e, used, prm["w_gate"], prm["w_up"], prm["w_down"])
    out = _combine(dest, h1, gate_col, p, prm["w_ple_proj"], prm["w_ple_gate"], prm["norm_ple"],
                   prm["norm_final"], y_sorted)
    return out.reshape(batch, seq, D_MODEL)


def _prepare(w_in, conv_w, conv_b, dt_bias, a_log, d_skip, ssd_norm, rel_bias, attn_norm, w_out, norm_mix,
             norm_ffn, router_group, router_expert, w_gate, w_up, w_down, norm_ple, w_ple_proj, w_ple_gate,
             norm_final):
    w_in = w_in[0]
    offs = np.cumsum([0, SSD_WIDTH, CONV_DIM, 2 * N_HEADS, D_MODEL, D_MODEL, D_MODEL])
    piece = lambda i: w_in[:, offs[i]:offs[i + 1]]
    w_dt = jnp.zeros((D_MODEL, 2 * LANES), F32)
    w_dt = w_dt.at[:, :N_HEADS].set(piece(2)[:, :N_HEADS]).at[:, LANES:LANES + N_HEADS].set(piece(2)[:, N_HEADS:])
    row = lambda v: v.astype(F32).reshape(1, -1)
    w_router = jnp.zeros((ROUTER_ROWS, D_MODEL), F32)
    w_router = w_router.at[:N_GROUPS].set(router_group[0].T)
    w_router = w_router.at[8:8 + N_EXPERTS].set(router_expert[0].transpose(0, 2, 1).reshape(N_EXPERTS, D_MODEL))
    tm = TOKEN_TILE
    upper = jnp.asarray(np.triu(np.ones((tm, tm), np.float32), k=1), BF16)
    return {
        "norm_mix": row(norm_mix[0]),
        "w_z": piece(0).astype(BF16), "w_xbc": piece(1).astype(BF16), "w_dt": w_dt.astype(BF16),
        "w_q": piece(3).astype(BF16), "w_k": piece(4).astype(BF16), "w_v": piece(5).astype(BF16),
        "conv_w": jnp.zeros((8, CONV_DIM), F32).at[:D_CONV].set(conv_w[0]),
        "conv_b": row(conv_b[0]),
        "dt_bias": [_pad_lanes(dt_bias[0, d]) for d in range(2)],
        "a_log": [jnp.full((1, LANES), -jnp.inf, F32).at[0, :N_HEADS].set(a_log[0, d]) for d in range(2)],
        "d_exp": jnp.repeat(d_skip[0].astype(F32), HEAD_DIM).reshape(1, SSD_WIDTH),
        "ssd_norm": row(ssd_norm[0]),
        "bias_tab": _bias_table(rel_bias[0]),
        "attn_norm": row(attn_norm[0]),
        "w_out_a": w_out[0, :SSD_WIDTH].astype(BF16), "w_out_b": w_out[0, SSD_WIDTH:].astype(BF16),
        "norm_ffn": row(norm_ffn[0]),
        "w_router": w_router,
        "upper": upper,
        "w_gate": w_gate[0], "w_up": w_up[0], "w_down": w_down[0],
        "norm_ple": row(norm_ple[0]),
        "w_ple_proj": w_ple_proj[0].astype(BF16), "w_ple_gate": w_ple_gate[0].astype(BF16),
        "norm_final": row(norm_final),
    }


def kernel(x_prompt, x_sample, p_prompt, p_sample, w_in, conv_w, conv_b, dt_bias, a_log, d_skip, ssd_norm,
           rel_bias, attn_norm, w_out, norm_mix, norm_ffn, router_group, router_expert, w_gate, w_up, w_down,
           norm_ple, w_ple_proj, w_ple_gate, norm_final):
    prm = _prepare(w_in, conv_w, conv_b, dt_bias, a_log, d_skip, ssd_norm, rel_bias, attn_norm, w_out, norm_mix,
                   norm_ffn, router_group, router_expert, w_gate, w_up, w_down, norm_ple, w_ple_proj, w_ple_gate,
                   norm_final)
    y_prompt = _encoder(x_prompt, p_prompt[0], prm)
    y_sample = _encoder(x_sample, p_sample[0], prm)
    return (y_prompt, y_sample)
```

```python
import functools
import math

import numpy as np
import jax
import jax.numpy as jnp
from jax import lax
from jax.experimental import pallas as pl
from jax.experimental.pallas import tpu as pltpu

F32 = jnp.float32
BF16 = jnp.bfloat16

D_MODEL = 1024
SSD_WIDTH = 1024
HEAD_DIM = 64
N_HEADS = 16
N_BC_GROUPS = 2
D_STATE = 128
D_CONV = 5
CHUNK = 128
CONV_DIM = SSD_WIDTH + 2 * N_BC_GROUPS * D_STATE
HEADS_PER_GROUP = N_HEADS // N_BC_GROUPS
GROUP_WIDTH = SSD_WIDTH // N_BC_GROUPS
GRID_W = 64
WIN_H = 8
WIN_W = 16
N_GROUPS = 4
EXPERTS_PER_GROUP = 8
N_EXPERTS = N_GROUPS * EXPERTS_PER_GROUP
D_EXPERT = 512
PLE_DIM = 256
EPS = 1e-6
LOG2E = math.log2(math.e)

LANES = 128
HALO_ROWS = 16
TOKEN_TILE = 512
SSD_TILE = 1024
CONV_COLS = 512
SSD_TRIP_CHUNKS = 8
NA_ROWS = 16
NA_HALO_ROWS = 4
NA_TRIP_ROWS = 8
SLOT_BLOCK = 512
DISPATCH_TILE = 1024
COMBINE_TILE = 512
ROW_GROUP = 8
ROW_SUB = 8
ROUTER_ROWS = 48
VMEM_LIMIT = 56 * 1024 * 1024


def _rms(x, g):
    ms = jnp.mean(x * x, axis=-1, keepdims=True)
    return x * lax.rsqrt(ms + EPS) * g


def _split3(x):
    hi = x.astype(BF16)
    r1 = x - hi.astype(F32)
    mid = r1.astype(BF16)
    lo = (r1 - mid.astype(F32)).astype(BF16)
    return hi, mid, lo


def _dot(a, b):
    return jnp.dot(a, b, preferred_element_type=F32)


def _dot_nt(a, b):
    return lax.dot_general(a, b, (((1,), (1,)), ((), ())), preferred_element_type=F32)


def _tiled(m):
    return (m * ROW_SUB, LANES)


def _dot_tn(a, b):
    return lax.dot_general(a, b, (((0,), (0,)), ((), ())), preferred_element_type=F32)


def _store_row_tiles(ref, x):
    m = x.shape[0]
    for s in range(ROW_SUB):
        ref[pl.ds(s, m, stride=ROW_SUB), :] = x[:, s * LANES:(s + 1) * LANES]


def _load_row_tiles(ref):
    m = ref.shape[0] // ROW_SUB
    return jnp.concatenate([ref[pl.ds(s, m, stride=ROW_SUB), :] for s in range(ROW_SUB)], axis=1)


def _params(sem):
    return pltpu.CompilerParams(dimension_semantics=sem, vmem_limit_bytes=VMEM_LIMIT)


def _inproj_kernel(x_ref, g_ref, wz_ref, wxbc_ref, wq_ref, wk_ref, wv_ref, wdt_ref,
                   z_ref, xbc_ref, q_ref, k_ref, v_ref, dt_ref):
    u = _rms(x_ref[...], g_ref[...]).astype(BF16)
    z_ref[...] = _dot(u, wz_ref[...]).astype(BF16)
    xbc_ref[...] = _dot(u, wxbc_ref[...]).astype(BF16)
    q_ref[...] = (_dot(u, wq_ref[...]) * (HEAD_DIM ** -0.5 * LOG2E)).astype(BF16)
    k_ref[...] = _dot(u, wk_ref[...]).astype(BF16)
    v_ref[...] = _dot(u, wv_ref[...]).astype(BF16)
    dt_ref[...] = _dot(u, wdt_ref[...])


def _inproj(x, g, wz, wxbc, wq, wk, wv, wdt):
    n = x.shape[0]
    tm = TOKEN_TILE
    row = lambda w: pl.BlockSpec((tm, w), lambda i: (i, 0))
    full = lambda a: pl.BlockSpec(a.shape, lambda i: (0,) * a.ndim)
    outs = [(D_MODEL, BF16), (CONV_DIM, BF16), (D_MODEL, BF16), (D_MODEL, BF16), (D_MODEL, BF16),
            (LANES, F32)]
    return pl.pallas_call(
        _inproj_kernel,
        grid=(n // tm,),
        in_specs=[row(D_MODEL), full(g), full(wz), full(wxbc), full(wq), full(wk), full(wv), full(wdt)],
        out_specs=[row(w) for w, _ in outs],
        out_shape=[jax.ShapeDtypeStruct((n, w), d) for w, d in outs],
        compiler_params=_params(("parallel",)),
        name="inproj",
    )(x, g, wz, wxbc, wq, wk, wv, wdt)


def _ssd_kernel(*refs, backward, n_blocks):
    if backward:
        (xbc_ref, prev_ref, next_ref, dt_ref, cw_ref, cb_ref, dtb_ref, alog_ref,
         out_ref, act_ref, xp_ref, st_ref) = refs
    else:
        (act_ref, dt_ref, dtb_ref, alog_ref, z_ref, ybwd_ref, dexp_ref, gn_ref, out_ref, st_ref) = refs
    tb = act_ref.shape[0]
    t = CHUNK
    lane0 = N_HEADS if backward else 0
    n_chunks = tb // t
    j = pl.program_id(1)

    @pl.when(j == 0)
    def _():
        st_ref[...] = jnp.zeros_like(st_ref)

    if backward:
        blk = n_blocks - 1 - j
        hr = HALO_ROWS
        xp_ref[0:hr, :] = jnp.where(blk == 0, jnp.zeros_like(prev_ref), prev_ref[...])
        xp_ref[hr:hr + tb, :] = xbc_ref[...]
        xp_ref[hr + tb:, :] = jnp.where(blk == n_blocks - 1, jnp.zeros_like(next_ref), next_ref[...])
        taps = [kk for kk in range(D_CONV) if kk != D_CONV // 2]
        win = t + 2 * hr
        srow = lax.broadcasted_iota(jnp.int32, (len(taps) * t, win), 0)
        scol = lax.broadcasted_iota(jnp.int32, (len(taps) * t, win), 1)
        tap_id = srow // t
        tap_off = tap_id + jnp.where(tap_id >= D_CONV // 2, 1, 0) + (hr - D_CONV // 2)
        shift_sel = jnp.where(scol == (srow % t) + tap_off, 1.0, 0.0).astype(BF16)
        for c in range(n_chunks):
            for c0 in range(0, CONV_DIM, CONV_COLS):
                cols = slice(c0, c0 + CONV_COLS)
                window = xp_ref[c * t:c * t + win, cols]
                shifted = _dot(shift_sel, window)
                acc = cb_ref[:, cols] + cw_ref[D_CONV // 2:D_CONV // 2 + 1, cols] * window[hr:hr + t].astype(F32)
                for ti, kk in enumerate(taps):
                    acc = acc + cw_ref[kk:kk + 1, cols] * shifted[ti * t:(ti + 1) * t]
                act_ref[c * t:(c + 1) * t, cols] = (acc * (1.0 / (1.0 + jnp.exp(-acc)))).astype(act_ref.dtype)

    a_row = -jnp.exp(alog_ref[...]) * LOG2E
    dtb = dtb_ref[...]
    ri = lax.broadcasted_iota(jnp.int32, (t, t), 0)
    ci = lax.broadcasted_iota(jnp.int32, (t, t), 1)
    if backward:
        keep = ci >= ri
    else:
        keep = ri >= ci
    tri = jnp.where(keep, 1.0, 0.0).astype(BF16)
    er = lax.broadcasted_iota(jnp.int32, (2 * LANES, SSD_WIDTH), 0)
    ec = lax.broadcasted_iota(jnp.int32, (2 * LANES, SSD_WIDTH), 1)
    expand = jnp.where((er % LANES) == (ec // HEAD_DIM) + lane0, 1.0, 0.0).astype(BF16)
    lane = lax.broadcasted_iota(jnp.int32, (t, LANES), 1)
    low_half = lane < HEAD_DIM

    def chunk(c, st):
        r0 = pl.multiple_of(c * t, t)
        act = act_ref[pl.ds(r0, t), :]
        xs_bf = act[:, :SSD_WIDTH]
        xs = xs_bf.astype(F32)
        groups = []
        for g in range(N_BC_GROUPS):
            b_g = act[:, SSD_WIDTH + g * D_STATE:SSD_WIDTH + (g + 1) * D_STATE]
            c_g = act[:, SSD_WIDTH + (N_BC_GROUPS + g) * D_STATE:SSD_WIDTH + (N_BC_GROUPS + g + 1) * D_STATE]
            groups.append((b_g, c_g, slice(g * GROUP_WIDTH, (g + 1) * GROUP_WIDTH)))

        st_bf = st.astype(BF16)
        cbs = [_dot_nt(c_g, b_g) for b_g, c_g, _ in groups]
        y_offs = [_dot(c_g, st_bf[:, hs]) for _, c_g, hs in groups]

        dt_raw = dt_ref[pl.ds(r0, t), :] + dtb
        dt = jnp.maximum(dt_raw, 0.0) + jnp.log(1.0 + jnp.exp(-jnp.abs(dt_raw)))
        hi, mid, lo = _split3(dt * a_row)
        cs = _dot(tri, hi) + _dot(tri, mid) + _dot(tri, lo)
        cs_row = cs.T
        dt_row = dt.T
        edge = cs[0:1, :] if backward else cs[t - 1:t, :]
        w_in = dt * jnp.exp2(edge - cs)
        ecs = jnp.exp2(cs)
        both = jnp.concatenate([w_in, ecs], axis=0)
        b_hi = both.astype(BF16)
        b_lo = (both - b_hi.astype(F32)).astype(BF16)
        both_x = _dot(jnp.concatenate([b_hi, b_lo], axis=1), expand)
        w_x = both_x[:t]
        ecs_x = both_x[t:]
        cd_x = ecs_x[0:1, :] if backward else ecs_x[t - 1:t, :]

        y_parts = []
        for g, (_, _, hs) in enumerate(groups):
            pair_parts = []
            for pp in range(HEADS_PER_GROUP // 2):
                h0 = g * HEADS_PER_GROUP + 2 * pp
                xs_pair = xs_bf[:, h0 * HEAD_DIM:(h0 + 2) * HEAD_DIM]
                ys = []
                for h in (h0, h0 + 1):
                    diff = cs[:, lane0 + h:lane0 + h + 1] - cs_row[lane0 + h:lane0 + h + 1, :]
                    seg = jnp.exp2(jnp.where(keep, diff, -jnp.inf))
                    m = (cbs[g] * seg * dt_row[lane0 + h:lane0 + h + 1, :]).astype(BF16)
                    ys.append(_dot(m, xs_pair))
                pair_parts.append(jnp.where(low_half, ys[0], ys[1]))
            y_parts.append(jnp.concatenate(pair_parts, axis=1) + y_offs[g] * ecs_x[:, hs])
        y = jnp.concatenate(y_parts, axis=1)

        xw = (xs * w_x).astype(BF16)
        news = [_dot_tn(b_g, xw[:, hs]) for b_g, _, hs in groups]
        st_new = st * cd_x + jnp.concatenate(news, axis=1)

        if backward:
            out_ref[pl.ds(r0, t), :] = y.astype(out_ref.dtype)
        else:
            y = y + ybwd_ref[pl.ds(r0, t), :].astype(F32) + dexp_ref[...] * xs
            zz = z_ref[pl.ds(r0, t), :].astype(F32)
            y = y * (zz * (1.0 / (1.0 + jnp.exp(-zz))))
            out_ref[pl.ds(r0, t), :] = _rms(y, gn_ref[...]).astype(out_ref.dtype)
        return st_new

    def trip(i, carry):
        st = st_ref[...]
        for u in range(SSD_TRIP_CHUNKS):
            ci = i * SSD_TRIP_CHUNKS + u
            st = chunk((n_chunks - 1 - ci) if backward else ci, st)
        st_ref[...] = st
        return carry

    lax.fori_loop(0, n_chunks // SSD_TRIP_CHUNKS, trip, 0)


def _ssd_pass(xin, dt, dtb, alog, *, batch, seq, backward, extra=()):
    n = xin.shape[0]
    tb = SSD_TILE
    nb = seq // tb
    hb = tb // HALO_ROWS
    n_halo = n // HALO_ROWS

    def blk_of(b, j):
        return b * nb + ((nb - 1 - j) if backward else j)

    main = lambda w: pl.BlockSpec((tb, w), lambda b, j: (blk_of(b, j), 0))
    full = lambda a: pl.BlockSpec(a.shape, lambda b, j: (0,) * a.ndim)
    dt_spec = pl.BlockSpec((tb, LANES), lambda b, j: (blk_of(b, j), 0))
    state = pltpu.VMEM((D_STATE, SSD_WIDTH), F32)
    y_shape = jax.ShapeDtypeStruct((n, SSD_WIDTH), BF16)
    if backward:
        cw, cb = extra
        in_specs = [
            main(CONV_DIM),
            pl.BlockSpec((HALO_ROWS, CONV_DIM), lambda b, j: (jnp.maximum(blk_of(b, j) * hb - 1, 0), 0)),
            pl.BlockSpec((HALO_ROWS, CONV_DIM), lambda b, j: (jnp.minimum((blk_of(b, j) + 1) * hb, n_halo - 1), 0)),
            dt_spec, full(cw), full(cb), full(dtb), full(alog),
        ]
        args = [xin, xin, xin, dt, cw, cb, dtb, alog]
        out_specs = [main(SSD_WIDTH), main(CONV_DIM)]
        out_shape = [y_shape, jax.ShapeDtypeStruct((n, CONV_DIM), BF16)]
        scratch = [pltpu.VMEM((tb + 2 * HALO_ROWS, CONV_DIM), BF16), state]
    else:
        z, ybwd, dexp, gn = extra
        in_specs = [main(CONV_DIM), dt_spec, full(dtb), full(alog), main(SSD_WIDTH), main(SSD_WIDTH), full(dexp),
                    full(gn)]
        args = [xin, dt, dtb, alog, z, ybwd, dexp, gn]
        out_specs = main(SSD_WIDTH)
        out_shape = y_shape
        scratch = [state]
    return pl.pallas_call(
        functools.partial(_ssd_kernel, backward=backward, n_blocks=nb),
        grid=(batch, nb),
        in_specs=in_specs,
        out_specs=out_specs,
        out_shape=out_shape,
        scratch_shapes=scratch,
        compiler_params=_params(("parallel", "arbitrary")),
        name="ssd_bwd" if backward else "ssd_fwd",
    )(*args)


def _natten_kernel(q_ref, kp_ref, kc_ref, kn_ref, vp_ref, vc_ref, vn_ref, bias_ref, gn_ref,
                   out_ref, kbuf, vbuf, obuf, *, rows):
    halo = NA_HALO_ROWS * GRID_W
    full = NA_ROWS * GRID_W
    jb = pl.program_id(1)
    kbuf[0:halo, :] = kp_ref[...]
    kbuf[halo:halo + full, :] = kc_ref[...]
    kbuf[halo + full:, :] = kn_ref[...]
    vbuf[0:halo, :] = vp_ref[...]
    vbuf[halo:halo + full, :] = vc_ref[...]
    vbuf[halo + full:, :] = vn_ref[...]
    lane = lax.broadcasted_iota(jnp.int32, (GRID_W, LANES), 1)
    low_half = lane < HEAD_DIM
    n_keys = WIN_H * GRID_W
    ones_keys = jnp.ones((n_keys, LANES), BF16)

    n_pairs = N_HEADS // 2
    lanes_of = lambda pp: slice(pp * LANES, (pp + 1) * LANES)

    def trip_body(it):
        geo = []
        for rr in range(NA_TRIP_ROWS):
            i = it * NA_TRIP_ROWS + rr
            r = jb * NA_ROWS + i
            s = jnp.clip(r - WIN_H // 2, 0, rows - WIN_H)
            local = s - (jb * NA_ROWS - NA_HALO_ROWS)
            geo.append((pl.multiple_of(local * GRID_W, GRID_W), pl.multiple_of(i * GRID_W, GRID_W),
                        s - r + (WIN_H - 1)))
        scores = []
        for k0, q0, d0 in geo:
            for pp in range(n_pairs):
                q_pair = q_ref[pl.ds(q0, GRID_W), lanes_of(pp)]
                zero = jnp.zeros_like(q_pair)
                q2 = jnp.concatenate([jnp.where(low_half, q_pair, zero), jnp.where(low_half, zero, q_pair)], axis=0)
                sc = _dot_nt(q2, kbuf[pl.ds(k0, n_keys), lanes_of(pp)])
                bias = jnp.concatenate(
                    [jnp.concatenate([bias_ref[d0 + 2 * ii, 2 * pp + hh] for ii in range(WIN_H // 2)], axis=1)
                     for hh in range(2)], axis=0)
                scores.append(sc + bias)
        probs = []
        for sc in scores:
            m = jnp.max(sc, axis=-1, keepdims=True)
            probs.append(jnp.exp2(sc - m).astype(BF16))
        for rr, (k0, q0, d0) in enumerate(geo):
            for pp in range(n_pairs):
                v_ext = jnp.concatenate([vbuf[pl.ds(k0, n_keys), lanes_of(pp)], ones_keys], axis=1)
                o_ext = _dot(probs[rr * n_pairs + pp], v_ext)
                o2 = o_ext[:, :LANES] * (1.0 / o_ext[:, LANES:])
                obuf[pl.ds(q0, GRID_W), lanes_of(pp)] = jnp.where(low_half, o2[:GRID_W], o2[GRID_W:])
        for rr, (k0, q0, d0) in enumerate(geo):
            out_ref[pl.ds(q0, GRID_W), :] = _rms(obuf[pl.ds(q0, GRID_W), :], gn_ref[...]).astype(out_ref.dtype)

    for it in range(NA_ROWS // NA_TRIP_ROWS):
        trip_body(it)


def _natten(q, k, v, bias_tab, gn, *, batch, seq):
    n = q.shape[0]
    rows = seq // GRID_W
    nrb = rows // NA_ROWS
    full = NA_ROWS * GRID_W
    halo = NA_HALO_ROWS * GRID_W
    per = NA_ROWS // NA_HALO_ROWS
    n_halo = n // halo
    cur = pl.BlockSpec((full, D_MODEL), lambda b, j: (b * nrb + j, 0))
    prev = pl.BlockSpec((halo, D_MODEL), lambda b, j: (jnp.maximum((b * nrb + j) * per - 1, 0), 0))
    nxt = pl.BlockSpec((halo, D_MODEL), lambda b, j: (jnp.minimum((b * nrb + j + 1) * per, n_halo - 1), 0))
    const = lambda a: pl.BlockSpec(a.shape, lambda b, j: (0,) * a.ndim)
    return pl.pallas_call(
        functools.partial(_natten_kernel, rows=rows),
        grid=(batch, nrb),
        in_specs=[cur, prev, cur, nxt, prev, cur, nxt, const(bias_tab), const(gn)],
        out_specs=cur,
        out_shape=jax.ShapeDtypeStruct((n, D_MODEL), BF16),
        scratch_shapes=[pltpu.VMEM((full + 2 * halo, D_MODEL), BF16), pltpu.VMEM((full + 2 * halo, D_MODEL), BF16),
                        pltpu.VMEM((full, D_MODEL), F32)],
        compiler_params=_params(("parallel", "parallel")),
        name="natten",
    )(q, k, k, k, v, v, v, bias_tab, gn)


def _bias_table(rel_bias):
    qc = np.arange(GRID_W)[:, None]
    kc = np.arange(GRID_W)[None, :]
    s_c = np.clip(qc - WIN_W // 2, 0, GRID_W - WIN_W)
    inside = (kc >= s_c) & (kc < s_c + WIN_W)
    rb = rel_bias.astype(F32)
    edge = GRID_W - WIN_W
    padded = jnp.concatenate([jnp.repeat(rb[..., :1], edge, axis=-1), rb, jnp.repeat(rb[..., -1:], edge, axis=-1)],
                             axis=-1)
    tab = jnp.stack([padded[..., GRID_W - 1 - q:2 * GRID_W - 1 - q] for q in range(GRID_W)], axis=2)
    tab = jnp.where(inside[None, None], tab * LOG2E, -jnp.inf)
    pairs = jnp.concatenate([tab[:, :-1], tab[:, 1:]], axis=-1)
    return pairs.transpose(1, 0, 2, 3)


def _outproj_kernel(x_ref, ys_ref, ya_ref, wa_ref, wb_ref, gn_ref, wr_ref, upper_ref,
                    h_ref, u_ref, idx_ref, gate_ref, cnt_ref, carry_ref):
    tm = x_ref.shape[0]

    @pl.when(pl.program_id(0) == 0)
    def _():
        carry_ref[...] = jnp.zeros_like(carry_ref)

    h = x_ref[...] + _dot(ys_ref[...], wa_ref[...]) + _dot(ya_ref[...], wb_ref[...])
    h_ref[...] = h
    u = _rms(h, gn_ref[...])
    _store_row_tiles(u_ref, u)

    wr = wr_ref[...]
    w_hi = wr.astype(BF16)
    w_lo = (wr - w_hi.astype(F32)).astype(BF16)
    both = _dot_nt(jnp.concatenate([w_hi, w_lo], axis=0), u.astype(BF16))
    logit = both[:ROUTER_ROWS] + both[ROUTER_ROWS:]

    lg = logit[0:N_GROUPS]
    gi = lax.broadcasted_iota(jnp.int32, (N_GROUPS, tm), 0)
    g_max = jnp.max(lg, axis=0, keepdims=True)
    g_idx = jnp.min(jnp.where(lg == g_max, gi, N_GROUPS), axis=0, keepdims=True)
    g_val = 1.0 / jnp.sum(jnp.exp(lg - g_max), axis=0, keepdims=True)

    le = logit[8:8 + EXPERTS_PER_GROUP]
    for g in range(1, N_GROUPS):
        le = jnp.where(g_idx == g, logit[8 + g * EXPERTS_PER_GROUP:8 + (g + 1) * EXPERTS_PER_GROUP], le)
    ei = lax.broadcasted_iota(jnp.int32, (EXPERTS_PER_GROUP, tm), 0)
    m1 = jnp.max(le, axis=0, keepdims=True)
    i1 = jnp.min(jnp.where(le == m1, ei, EXPERTS_PER_GROUP), axis=0, keepdims=True)
    le2 = jnp.where(ei == i1, -jnp.inf, le)
    m2 = jnp.max(le2, axis=0, keepdims=True)
    i2 = jnp.min(jnp.where(le2 == m2, ei, EXPERTS_PER_GROUP), axis=0, keepdims=True)
    e21 = jnp.exp(m2 - m1)
    den = 1.0 / (1.0 + e21)
    gate1 = g_val * den
    gate2 = g_val * e21 * den
    e1 = g_idx * EXPERTS_PER_GROUP + i1
    e2 = g_idx * EXPERTS_PER_GROUP + i2

    xi = lax.broadcasted_iota(jnp.int32, (N_EXPERTS, tm), 0)
    hot1 = xi == e1
    hot2 = xi == e2
    hot = jnp.where(jnp.logical_or(hot1, hot2), 1.0, 0.0)
    before = _dot(hot.astype(BF16), upper_ref[...])
    carry = carry_ref[...]
    before = before + jnp.tile(carry, (1, tm // LANES))
    rank1 = jnp.sum(jnp.where(hot1, before, 0.0), axis=0, keepdims=True)
    rank2 = jnp.sum(jnp.where(hot2, before, 0.0), axis=0, keepdims=True)
    carry = carry + jnp.sum(hot, axis=1, keepdims=True)
    carry_ref[...] = carry
    cnt_ref[...] = carry

    zi = jnp.zeros((4, tm), jnp.int32)
    idx_ref[...] = jnp.concatenate([e1, e2, rank1.astype(jnp.int32), rank2.astype(jnp.int32), zi], axis=0)
    gates = jnp.concatenate([gate1, gate2, jnp.zeros((LANES - 2, tm), F32)], axis=0)
    gate_ref[...] = gates.T


def _outproj(x, y_ssd, y_att, wa, wb, gn, wr, upper):
    n = x.shape[0]
    tm = TOKEN_TILE
    row = lambda w: pl.BlockSpec((tm, w), lambda i: (i, 0))
    full = lambda a: pl.BlockSpec(a.shape, lambda i: (0,) * a.ndim)
    return pl.pallas_call(
        _outproj_kernel,
        grid=(n // tm,),
        in_specs=[row(D_MODEL), row(D_MODEL), row(D_MODEL), full(wa), full(wb), full(gn), full(wr), full(upper)],
        out_specs=[row(D_MODEL), pl.BlockSpec(_tiled(tm), lambda i: (i, 0)),
                   pl.BlockSpec((8, tm), lambda i: (0, i)), row(LANES),
                   pl.BlockSpec((N_EXPERTS, LANES), lambda i: (0, 0))],
        out_shape=[jax.ShapeDtypeStruct((n, D_MODEL), F32), jax.ShapeDtypeStruct(_tiled(n), F32),
                   jax.ShapeDtypeStruct((8, n), jnp.int32), jax.ShapeDtypeStruct((n, LANES), F32),
                   jax.ShapeDtypeStruct((N_EXPERTS, LANES), F32)],
        scratch_shapes=[pltpu.VMEM((N_EXPERTS, LANES), F32)],
        compiler_params=_params(("arbitrary",)),
        name="outproj_router",
    )(x, y_ssd, y_att, wa, wb, gn, wr, upper)


def _rows(ref, first, count):
    return ref.at[pl.ds(pl.multiple_of(first * ROW_SUB, ROW_SUB), count * ROW_SUB)]


def _row_copy(src_ref, src_row, dst_ref, dst_row, sem):
    return pltpu.make_async_copy(_rows(src_ref, src_row, 1), _rows(dst_ref, dst_row, 1), sem)


def _for_each_choice(dest_ref, n_rows, fn, unrolled=False):
    if unrolled:
        for r in range(n_rows):
            for kk in range(2):
                fn(r, kk, dest_ref[0, 0, kk * n_rows + r])
        return

    def group(g, carry):
        base = pl.multiple_of(g * ROW_GROUP, ROW_GROUP)
        for jj in range(ROW_GROUP):
            for kk in range(2):
                fn(base + jj, kk, dest_ref[0, 0, kk * n_rows + base + jj])
        return carry

    lax.fori_loop(0, n_rows // ROW_GROUP, group, 0)


def _dispatch_kernel(pstart_ref, pend_ref, dest_ref, u_ref, slots_ref, zero_ref, sem):
    tm = u_ref.shape[0] // ROW_SUB

    @pl.when(pl.program_id(0) == 0)
    def _():
        zero_ref[...] = jnp.zeros_like(zero_ref)

        def fill(e):
            return pltpu.make_async_copy(zero_ref, _rows(slots_ref, pend_ref[e] - SLOT_BLOCK, SLOT_BLOCK), sem)

        for e in range(N_EXPERTS):
            @pl.when(pend_ref[e] > pstart_ref[e])
            def _():
                fill(e).start()
        for e in range(N_EXPERTS):
            @pl.when(pend_ref[e] > pstart_ref[e])
            def _():
                fill(e).wait()

        def tail(j):
            return pltpu.make_async_copy(zero_ref, _rows(slots_ref, j * SLOT_BLOCK, SLOT_BLOCK), sem)

        used = lax.div(pend_ref[N_EXPERTS - 1], SLOT_BLOCK)
        n_blocks = slots_ref.shape[0] // (SLOT_BLOCK * ROW_SUB)

        def tail_start(j, carry):
            tail(j).start()
            return carry

        def tail_wait(j, carry):
            tail(j).wait()
            return carry

        lax.fori_loop(used, n_blocks, tail_start, 0)
        lax.fori_loop(used, n_blocks, tail_wait, 0)

    _for_each_choice(dest_ref, tm, lambda r, kk, slot: _row_copy(u_ref, r, slots_ref, slot, sem).start(priority=kk))
    for _ in range(2):
        pltpu.make_async_copy(u_ref, _rows(slots_ref, 0, tm), sem).wait()


def _tile_slots(dest, tm):
    nt = dest.shape[1] // tm
    return dest.reshape(2, nt, tm).transpose(1, 0, 2).reshape(nt, 1, 2 * tm)


def _dispatch(u, dest, pad_start, pad_end, n_slots):
    n = u.shape[0] // ROW_SUB
    tm = DISPATCH_TILE
    dest = _tile_slots(dest, tm)
    grid_spec = pltpu.PrefetchScalarGridSpec(
        num_scalar_prefetch=2,
        grid=(n // tm,),
        in_specs=[pl.BlockSpec((1, 1, 2 * tm), lambda i, ps, pe: (i, 0, 0), memory_space=pltpu.SMEM),
                  pl.BlockSpec(_tiled(tm), lambda i, ps, pe: (i, 0))],
        out_specs=pl.BlockSpec(memory_space=pl.ANY),
        scratch_shapes=[pltpu.VMEM(_tiled(SLOT_BLOCK), F32), pltpu.SemaphoreType.DMA(())],
    )
    return pl.pallas_call(
        _dispatch_kernel,
        grid_spec=grid_spec,
        out_shape=jax.ShapeDtypeStruct(_tiled(n_slots), F32),
        compiler_params=_params(("arbitrary",)),
        name="dispatch",
    )(pad_start, pad_end, dest, u)


def _expert_kernel(blk_e_ref, used_ref, x_ref, wg_ref, wu_ref, wd_ref, y_ref, wg_bf, wu_bf, wd_bf):
    j = pl.program_id(0)

    @pl.when(jnp.logical_or(j == 0, blk_e_ref[j] != blk_e_ref[jnp.maximum(j - 1, 0)]))
    def _():
        wg_bf[...] = wg_ref[0].astype(BF16)
        wu_bf[...] = wu_ref[0].astype(BF16)
        wd_bf[...] = wd_ref[0].astype(BF16)

    @pl.when(j < used_ref[0])
    def _():
        x = _load_row_tiles(x_ref).astype(BF16)
        a = _dot(x, wg_bf[...])
        b = _dot(x, wu_bf[...])
        hid = (a * (1.0 / (1.0 + jnp.exp(-a))) * b).astype(BF16)
        _store_row_tiles(y_ref, _dot(hid, wd_bf[...]))

    @pl.when(pl.program_id(0) >= used_ref[0])
    def _():
        y_ref[...] = jnp.zeros_like(y_ref)


def _experts(slots, blk_e, used, wg, wu, wd):
    n_slots = slots.shape[0] // ROW_SUB
    nb = n_slots // SLOT_BLOCK
    xmap = lambda i, be, us: (jnp.minimum(i, us[0] - 1), 0)
    wmap = lambda i, be, us: (be[i], 0, 0)
    grid_spec = pltpu.PrefetchScalarGridSpec(
        num_scalar_prefetch=2,
        grid=(nb,),
        in_specs=[pl.BlockSpec(_tiled(SLOT_BLOCK), xmap),
                  pl.BlockSpec((1, D_MODEL, D_EXPERT), wmap),
                  pl.BlockSpec((1, D_MODEL, D_EXPERT), wmap),
                  pl.BlockSpec((1, D_EXPERT, D_MODEL), wmap)],
        out_specs=pl.BlockSpec(_tiled(SLOT_BLOCK), lambda i, be, us: (i, 0)),
        scratch_shapes=[pltpu.VMEM((D_MODEL, D_EXPERT), BF16), pltpu.VMEM((D_MODEL, D_EXPERT), BF16),
                        pltpu.VMEM((D_EXPERT, D_MODEL), BF16)],
    )
    return pl.pallas_call(
        _expert_kernel,
        grid_spec=grid_spec,
        out_shape=jax.ShapeDtypeStruct(_tiled(n_slots), F32),
        compiler_params=_params(("arbitrary",)),
        name="experts",
    )(blk_e, used, slots, wg, wu, wd)


def _combine_kernel(dest_ref, dest_next_ref, h_ref, gate_ref, p_ref, wp_ref, wg_ref, gp_ref,
                    gf_ref, y_ref, out_ref, buf, sems):
    tm = h_ref.shape[0]
    i = pl.program_id(0)

    def gather(dref, which, unrolled):
        _for_each_choice(dref, tm, lambda r, kk, slot: _row_copy(y_ref, slot, buf.at[which, kk], r,
                                                                 sems.at[which]).start(priority=kk), unrolled)

    def wait_tile(which):
        for kk in range(2):
            pltpu.make_async_copy(_rows(y_ref, 0, tm), buf.at[which, kk], sems.at[which]).wait()

    @pl.when(i == 0)
    def _():
        gather(dest_ref, 0, False)

    cur = i % 2
    wait_tile(cur)
    gather(dest_next_ref, 1 - cur, True)

    gates = gate_ref[...]
    moe = gates[:, 0:1] * _load_row_tiles(buf.at[cur, 0]) + gates[:, 1:2] * _load_row_tiles(buf.at[cur, 1])
    h = h_ref[...] + moe
    u = _rms(h, gp_ref[...]).astype(BF16)
    gate = 1.0 / (1.0 + jnp.exp(-_dot(u, wg_ref[...])))
    h = h + _dot(p_ref[...].astype(BF16), wp_ref[...]) * gate
    out_ref[...] = _rms(h, gf_ref[...])

    @pl.when(i + 1 == pl.num_programs(0))
    def _():
        wait_tile(1 - cur)


def _combine(dest, h, gate_col, p, wp, wg, gp, gf, y_sorted):
    n = h.shape[0]
    tm = COMBINE_TILE
    nt = n // tm
    dest = _tile_slots(dest, tm)
    row = lambda w: pl.BlockSpec((tm, w), lambda i: (i, 0))
    full = lambda a: pl.BlockSpec(a.shape, lambda i: (0,) * a.ndim)
    return pl.pallas_call(
        _combine_kernel,
        grid=(nt,),
        in_specs=[pl.BlockSpec((1, 1, 2 * tm), lambda i: (i, 0, 0), memory_space=pltpu.SMEM),
                  pl.BlockSpec((1, 1, 2 * tm), lambda i: (jnp.minimum(i + 1, nt - 1), 0, 0), memory_space=pltpu.SMEM),
                  row(D_MODEL), row(LANES), row(PLE_DIM), full(wp), full(wg), full(gp), full(gf),
                  pl.BlockSpec(memory_space=pl.ANY)],
        out_specs=row(D_MODEL),
        out_shape=jax.ShapeDtypeStruct((n, D_MODEL), F32),
        scratch_shapes=[pltpu.VMEM((2, 2) + _tiled(tm), F32), pltpu.SemaphoreType.DMA((2,))],
        compiler_params=_params(("arbitrary",)),
        name="combine_ple",
    )(dest, dest, h, gate_col, p, wp, wg, gp, gf, y_sorted)


def _head_lanes(v, direction, fill=0.0):
    lane0 = direction * N_HEADS
    return jnp.full((1, LANES), fill, F32).at[0, lane0:lane0 + N_HEADS].set(v.astype(F32))


def _encoder(x3, p3, prm):
    batch, seq, _ = x3.shape
    n = batch * seq
    assert seq % SSD_TILE == 0 and (seq // GRID_W) % NA_ROWS == 0 and n % TOKEN_TILE == 0
    x = x3.reshape(n, D_MODEL)
    p = p3.reshape(n, PLE_DIM)

    z, xbc, q, k, v, dt = _inproj(x, prm["norm_mix"], prm["w_z"], prm["w_xbc"], prm["w_q"], prm["w_k"],
                                  prm["w_v"], prm["w_dt"])

    y_bwd, act = _ssd_pass(xbc, dt, prm["dt_bias"][1], prm["a_log"][1], batch=batch, seq=seq, backward=True,
                           extra=(prm["conv_w"], prm["conv_b"]))
    y_ssd = _ssd_pass(act, dt, prm["dt_bias"][0], prm["a_log"][0], batch=batch, seq=seq, backward=False,
                      extra=(z, y_bwd, prm["d_exp"], prm["ssd_norm"]))
    y_att = _natten(q, k, v, prm["bias_tab"], prm["attn_norm"], batch=batch, seq=seq)

    h1, u2, route, gate_col, counts = _outproj(x, y_ssd, y_att, prm["w_out_a"], prm["w_out_b"], prm["norm_ffn"],
                                               prm["w_router"], prm["upper"])

    cnt = counts[:, 0].astype(jnp.int32)
    padded = (cnt + SLOT_BLOCK - 1) // SLOT_BLOCK * SLOT_BLOCK
    pad_end = jnp.cumsum(padded)
    pad_start = pad_end - padded
    n_blocks = -(-(2 * n + N_EXPERTS * (SLOT_BLOCK - 1)) // SLOT_BLOCK)
    n_slots = n_blocks * SLOT_BLOCK
    first_slot = jnp.arange(n_blocks, dtype=jnp.int32) * SLOT_BLOCK
    blk_e = jnp.minimum(jnp.sum((pad_end[None, :] <= first_slot[:, None]).astype(jnp.int32), axis=1), N_EXPERTS - 1)
    used = pad_end[-1:] // SLOT_BLOCK
    hot = route[0:2][None] == jnp.arange(N_EXPERTS, dtype=jnp.int32)[:, None, None]
    dest = route[2:4] + jnp.sum(jnp.where(hot, pad_start[:, None, None], 0), axis=0)

    slots = _dispatch(u2, dest, pad_start, pad_end, n_slots)
    y_sorted = _experts(slots, blk_e, used, prm["w_gate"], prm["w_up"], prm["w_down"])
    out = _combine(dest, h1, gate_col, p, prm["w_ple_proj"], prm["w_ple_gate"], prm["norm_ple"],
                   prm["norm_final"], y_sorted)
    return out.reshape(batch, seq, D_MODEL)


def _prepare(w_in, conv_w, conv_b, dt_bias, a_log, d_skip, ssd_norm, rel_bias, attn_norm, w_out, norm_mix,
             norm_ffn, router_group, router_expert, w_gate, w_up, w_down, norm_ple, w_ple_proj, w_ple_gate,
             norm_final):
    w_in = w_in[0]
    offs = np.cumsum([0, SSD_WIDTH, CONV_DIM, 2 * N_HEADS, D_MODEL, D_MODEL, D_MODEL])
    piece = lambda i: w_in[:, offs[i]:offs[i + 1]]
    w_dt = jnp.zeros((D_MODEL, LANES), F32).at[:, :2 * N_HEADS].set(piece(2))
    row = lambda v: v.astype(F32).reshape(1, -1)
    w_router = jnp.zeros((ROUTER_ROWS, D_MODEL), F32)
    w_router = w_router.at[:N_GROUPS].set(router_group[0].T)
    w_router = w_router.at[8:8 + N_EXPERTS].set(router_expert[0].transpose(0, 2, 1).reshape(N_EXPERTS, D_MODEL))
    tm = TOKEN_TILE
    upper = jnp.asarray(np.triu(np.ones((tm, tm), np.float32), k=1), BF16)
    return {
        "norm_mix": row(norm_mix[0]),
        "w_z": piece(0).astype(BF16), "w_xbc": piece(1).astype(BF16), "w_dt": w_dt.astype(BF16),
        "w_q": piece(3).astype(BF16), "w_k": piece(4).astype(BF16), "w_v": piece(5).astype(BF16),
        "conv_w": jnp.zeros((8, CONV_DIM), F32).at[:D_CONV].set(conv_w[0]),
        "conv_b": row(conv_b[0]),
        "dt_bias": [_head_lanes(dt_bias[0, d], d) for d in range(2)],
        "a_log": [_head_lanes(a_log[0, d], d, fill=-jnp.inf) for d in range(2)],
        "d_exp": jnp.repeat(d_skip[0].astype(F32), HEAD_DIM).reshape(1, SSD_WIDTH),
        "ssd_norm": row(ssd_norm[0]),
        "bias_tab": _bias_table(rel_bias[0]),
        "attn_norm": row(attn_norm[0]),
        "w_out_a": w_out[0, :SSD_WIDTH].astype(BF16), "w_out_b": w_out[0, SSD_WIDTH:].astype(BF16),
        "norm_ffn": row(norm_ffn[0]),
        "w_router": w_router,
        "upper": upper,
        "w_gate": w_gate[0], "w_up": w_up[0], "w_down": w_down[0],
        "norm_ple": row(norm_ple[0]),
        "w_ple_proj": w_ple_proj[0].astype(BF16), "w_ple_gate": w_ple_gate[0].astype(BF16),
        "norm_final": row(norm_final),
    }


def kernel(x_prompt, x_sample, p_prompt, p_sample, w_in, conv_w, conv_b, dt_bias, a_log, d_skip, ssd_norm,
           rel_bias, attn_norm, w_out, norm_mix, norm_ffn, router_group, router_expert, w_gate, w_up, w_down,
           norm_ple, w_ple_proj, w_ple_gate, norm_final):
    prm = _prepare(w_in, conv_w, conv_b, dt_bias, a_log, d_skip, ssd_norm, rel_bias, attn_norm, w_out, norm_mix,
                   norm_ffn, router_group, router_expert, w_gate, w_up, w_down, norm_ple, w_ple_proj, w_ple_gate,
                   norm_final)
    y_prompt = _encoder(x_prompt, p_prompt[0], prm)
    y_sample = _encoder(x_sample, p_sample[0], prm)
    return (y_prompt, y_sample)
```
